```python
import math
import jax, jax.numpy as jnp
from jax import lax
import numpy as np

D_MODEL = 4096
BATCH = 4
SEQ = 4096
DEPTH = 2

GRID_W = 64
CTX_LEN = 256
CHUNK = 128
A_HEADS = 8
A_WIDTH = D_MODEL // 2
A_HEAD_DIM = A_WIDTH // A_HEADS
B_HEADS = 8
B_HEAD_DIM = 128
B_V_DIM = 2 * B_HEAD_DIM
B_WIDTH = B_HEADS * B_V_DIM
MIX_WIDTH = A_WIDTH + B_WIDTH
QK_WIDTH = B_HEADS * 2 * B_HEAD_DIM
Q_OFF = 2 * A_WIDTH
K_OFF = Q_OFF + QK_WIDTH
VAL_OFF = K_OFF + QK_WIDTH
IN_COLS = VAL_OFF + B_WIDTH
N_EXPERTS = 32
TOP_K = 4
D_EXPERT = 640
SWIGLU_LIMIT = 7.0
SWIGLU_ALPHA = 1.702
MOE_BLOCK = 128
Q_BLOCK = 128
ROPE_THETA = 10000.0
EPS = 1e-6

kernel_name = 'hybrid_gmlp_diffattn_moe_dit'


def rms_norm(x, g):
    xf = x.astype(jnp.float32)
    y = xf * lax.rsqrt(jnp.mean(xf * xf, axis=-1, keepdims=True) + EPS)
    return (y * g.astype(jnp.float32)).astype(x.dtype)


def layer_norm(x, g, b):
    xf = x.astype(jnp.float32)
    xc = xf - jnp.mean(xf, axis=-1, keepdims=True)
    y = xc * lax.rsqrt(jnp.mean(xc * xc, axis=-1, keepdims=True) + EPS)
    return (y * g.astype(jnp.float32) + b.astype(jnp.float32)).astype(x.dtype)


def modulate(x, shift, scale):
    return x * (1 + scale) + shift


def axial_rope(n_tokens, dtype):
    rows = n_tokens // GRID_W
    row = jnp.repeat(jnp.arange(rows, dtype=jnp.float32), GRID_W)
    col = jnp.tile(jnp.arange(GRID_W, dtype=jnp.float32), rows)
    axis_dim = B_HEAD_DIM // 2
    inv_freq = jnp.power(ROPE_THETA, -jnp.arange(0, axis_dim, 2, dtype=jnp.float32) / axis_dim)
    ang_r = row[:, None] * inv_freq
    ang_c = col[:, None] * inv_freq
    ang = jnp.concatenate([ang_r, ang_r, ang_c, ang_c], axis=-1)
    return jnp.cos(ang).astype(dtype), jnp.sin(ang).astype(dtype)


def rot_half_axial(t):
    t1, t2, t3, t4 = jnp.split(t, 4, axis=-1)
    return jnp.concatenate([-t2, t1, -t4, t3], axis=-1)


def apply_rope(t, cos, sin):
    return t * cos[:, None, None, :] + rot_half_axial(t) * sin[:, None, None, :]


def qk_heads(t):
    return t.reshape(*t.shape[:-1], B_HEADS, 2, B_HEAD_DIM)


def v_heads(t):
    return t.reshape(*t.shape[:-1], B_HEADS, B_V_DIM)


def chunk_gmlp(z, w_s, b_s, ln_g, ln_b):
    bsz, n, _ = z.shape
    u, v = jnp.split(jax.nn.gelu(z), 2, axis=-1)
    v = layer_norm(v.reshape(bsz, n, A_HEADS, A_HEAD_DIM), ln_g.reshape(A_HEADS, A_HEAD_DIM), ln_b.reshape(A_HEADS, A_HEAD_DIM))
    v = v.reshape(bsz, n // CHUNK, CHUNK, A_HEADS, A_HEAD_DIM)
    mixed = jnp.einsum('hpq,bnqhc->bnphc', w_s, v) + b_s.T[:, :, None]
    return u * mixed.reshape(bsz, n, A_WIDTH)


def diff_attend(q, k, v, lam):
    s = jnp.einsum('bqhcd,bkhcd->bhcqk', q, k, preferred_element_type=jnp.float32) * (B_HEAD_DIM ** -0.5)
    p = jax.nn.softmax(s, axis=-1)
    a = p[:, :, 0] - lam * p[:, :, 1]
    return jnp.einsum('bhqk,bkhe->bqhe', a.astype(v.dtype), v)


def diff_attention_latent(q, k_lat, v_lat, k_ctx, v_ctx, lam):
    bsz, n = q.shape[:2]
    k = jnp.concatenate([k_ctx, k_lat], axis=1)
    v = jnp.concatenate([v_ctx, v_lat], axis=1)
    qb = jnp.moveaxis(q.reshape(bsz, n // Q_BLOCK, Q_BLOCK, B_HEADS, 2, B_HEAD_DIM), 1, 0)
    o = lax.map(lambda qi: diff_attend(qi, k, v, lam), qb)
    return jnp.moveaxis(o, 0, 1).reshape(bsz, n, B_HEADS, B_V_DIM)


def diff_head_out(o, g, lam_init):
    o = rms_norm(o, g) * (1.0 - lam_init)
    return o.reshape(*o.shape[:-2], B_WIDTH)


def moe_ffn(h, router_w, router_b, w_gu, b_gu, w_dn, b_dn):
    n_tok, d = h.shape
    logits = jnp.dot(h, router_w, preferred_element_type=jnp.float32) + router_b.astype(jnp.float32)
    top_v, top_i = lax.top_k(logits, TOP_K)
    gates = jax.nn.softmax(top_v, axis=-1)
    n_assign = n_tok * TOP_K
    flat_e = top_i.reshape(-1)
    flat_tok = jnp.arange(n_assign, dtype=jnp.int32) // TOP_K
    flat_g = gates.reshape(-1)
    order = jnp.argsort(flat_e)
    e_sorted = flat_e[order]
    counts = jnp.bincount(flat_e, length=N_EXPERTS)
    padded = (counts + MOE_BLOCK - 1) // MOE_BLOCK * MOE_BLOCK
    pad_end = jnp.cumsum(padded)
    pad_start = pad_end - padded
    start = jnp.cumsum(counts) - counts
    dest = pad_start[e_sorted] + jnp.arange(n_assign, dtype=jnp.int32) - start[e_sorted]
    n_blk = -(-n_assign // MOE_BLOCK) + N_EXPERTS
    n_rows = n_blk * MOE_BLOCK
    row_tok = jnp.zeros((n_rows,), jnp.int32).at[dest].set(flat_tok[order])
    row_gate = jnp.zeros((n_rows,), jnp.float32).at[dest].set(flat_g[order])
    blk_e = jnp.minimum(jnp.searchsorted(pad_end, jnp.arange(n_blk, dtype=jnp.int32) * MOE_BLOCK, side='right'), N_EXPERTS - 1)

    def expert_block(args):
        e, tok, g = args
        xb = h[tok]
        gu = xb @ w_gu[e] + b_gu[e]
        gate, up = gu[:, 0::2], gu[:, 1::2]
        gate = jnp.minimum(gate, SWIGLU_LIMIT)
        up = jnp.clip(up, -SWIGLU_LIMIT, SWIGLU_LIMIT)
        act = (up + 1) * (gate * jax.nn.sigmoid(SWIGLU_ALPHA * gate))
        out = act @ w_dn[e] + b_dn[e]
        return out * g[:, None].astype(out.dtype)

    outs = lax.map(expert_block, (blk_e, row_tok.reshape(n_blk, MOE_BLOCK), row_gate.reshape(n_blk, MOE_BLOCK)))
    return jax.ops.segment_sum(outs.reshape(n_rows, d), row_tok, num_segments=n_tok)


def setup_inputs(seed: int = 0) -> dict:
    key = jax.random.key(seed)
    ks = jax.random.split(key, 24)
    f32 = jnp.float32
    D = D_MODEL

    def nrm(k, shape, scale):
        return jax.random.normal(k, shape, f32) * scale

    return {
        'x': nrm(ks[0], (BATCH, SEQ, D), 1.0),
        'c': nrm(ks[1], (BATCH, D), 1.0),
        'ctx': nrm(ks[2], (BATCH, CTX_LEN, D), 1.0),
        'c_ctx': nrm(ks[3], (D,), 1.0),
        'w_mod': nrm(ks[4], (DEPTH, D, 6 * D), 0.5 * D ** -0.5),
        'b_mod': nrm(ks[5], (DEPTH, 6 * D), 0.02),
        'norm_g': 1.0 + nrm(ks[6], (DEPTH, 4, D), 0.05),
        'w_in': nrm(ks[7], (DEPTH, D, IN_COLS), D ** -0.5),
        'gmlp_ln_g': 1.0 + nrm(ks[8], (DEPTH, A_WIDTH), 0.05),
        'gmlp_ln_b': nrm(ks[9], (DEPTH, A_WIDTH), 0.02),
        'w_spatial': nrm(ks[10], (DEPTH, A_HEADS, CHUNK, CHUNK), CHUNK ** -0.5),
        'b_spatial': 1.0 + nrm(ks[11], (DEPTH, A_HEADS, CHUNK), 0.02),
        'lambda_q': nrm(ks[12], (DEPTH, 2, B_HEAD_DIM), 0.1),
        'lambda_k': nrm(ks[13], (DEPTH, 2, B_HEAD_DIM), 0.1),
        'subln_g': 1.0 + nrm(ks[14], (DEPTH, B_V_DIM), 0.05),
        'w_out': nrm(ks[15], (DEPTH, MIX_WIDTH, D), MIX_WIDTH ** -0.5),
        'router_w': nrm(ks[16], (DEPTH, D, N_EXPERTS), D ** -0.5),
        'router_b': nrm(ks[17], (DEPTH, N_EXPERTS), 0.01),
        'w_gate_up': nrm(ks[18], (DEPTH, N_EXPERTS, D, 2 * D_EXPERT), D ** -0.5),
        'b_gate_up': nrm(ks[19], (DEPTH, N_EXPERTS, 2 * D_EXPERT), 0.01),
        'w_down': nrm(ks[20], (DEPTH, N_EXPERTS, D_EXPERT, D), D_EXPERT ** -0.5),
        'b_down': nrm(ks[21], (DEPTH, N_EXPERTS, D), 0.01),
    }


def reference(x, c, ctx, c_ctx, w_mod, b_mod, norm_g, w_in, gmlp_ln_g, gmlp_ln_b, w_spatial, b_spatial, lambda_q, lambda_k, subln_g, w_out, router_w, router_b, w_gate_up, b_gate_up, w_down, b_down):
    bsz, n_lat, d = x.shape
    n_ctx = ctx.shape[1]
    cos, sin = axial_rope(n_lat, x.dtype)
    y = ctx
    for l in range(DEPTH):
        last = l == DEPTH - 1
        lam_init = 0.8 - 0.6 * math.exp(-0.3 * l)
        lq = lambda_q[l].astype(jnp.float32)
        lk = lambda_k[l].astype(jnp.float32)
        lam = jnp.exp(jnp.sum(lq[0] * lk[0])) - jnp.exp(jnp.sum(lq[1] * lk[1])) + lam_init

        mod = jax.nn.silu(c) @ w_mod[l] + b_mod[l]
        mod_c = jax.nn.silu(c_ctx) @ w_mod[l] + b_mod[l]
        sh1, sc1, g1, sh2, sc2, g2 = jnp.split(mod[:, None, :], 6, axis=-1)
        csh1, csc1, cg1, csh2, csc2, cg2 = jnp.split(mod_c, 6)

        h = modulate(rms_norm(x, norm_g[l, 0]), sh1, sc1)
        z = h @ w_in[l]
        hc = modulate(rms_norm(y, norm_g[l, 0]), csh1, csc1)
        if last:
            zc = hc @ w_in[l, :, K_OFF:]
            k_c = qk_heads(zc[..., :QK_WIDTH])
            v_c = v_heads(zc[..., QK_WIDTH:])
        else:
            zc = hc @ w_in[l]
            k_c = qk_heads(zc[..., K_OFF:VAL_OFF])
            v_c = v_heads(zc[..., VAL_OFF:])

        a_lat = chunk_gmlp(z[..., :Q_OFF], w_spatial[l], b_spatial[l], gmlp_ln_g[l], gmlp_ln_b[l])
        q = apply_rope(qk_heads(z[..., Q_OFF:K_OFF]), cos, sin)
        k = apply_rope(qk_heads(z[..., K_OFF:VAL_OFF]), cos, sin)
        val = v_heads(z[..., VAL_OFF:])
        b_lat = diff_head_out(diff_attention_latent(q, k, val, k_c, v_c, lam), subln_g[l], lam_init)
        mix = jnp.concatenate([a_lat, b_lat], axis=-1) @ w_out[l]
        x = x + g1 * rms_norm(mix, norm_g[l, 1])

        if not last:
            a_c = chunk_gmlp(zc[..., :Q_OFF], w_spatial[l], b_spatial[l], gmlp_ln_g[l], gmlp_ln_b[l])
            b_c = diff_head_out(diff_attend(qk_heads(zc[..., Q_OFF:K_OFF]), k_c, v_c, lam), subln_g[l], lam_init)
            mix_c = jnp.concatenate([a_c, b_c], axis=-1) @ w_out[l]
            y = y + cg1 * rms_norm(mix_c, norm_g[l, 1])

        h2 = modulate(rms_norm(x, norm_g[l, 2]), sh2, sc2).reshape(bsz * n_lat, d)
        if last:
            f_lat = moe_ffn(h2, router_w[l], router_b[l], w_gate_up[l], b_gate_up[l], w_down[l], b_down[l])
        else:
            hc2 = modulate(rms_norm(y, norm_g[l, 2]), csh2, csc2).reshape(bsz * n_ctx, d)
            f_all = moe_ffn(jnp.concatenate([hc2, h2], axis=0), router_w[l], router_b[l], w_gate_up[l], b_gate_up[l], w_down[l], b_down[l])
            f_c = f_all[:bsz * n_ctx].reshape(bsz, n_ctx, d)
            f_lat = f_all[bsz * n_ctx:]
            y = y + cg2 * rms_norm(f_c, norm_g[l, 3])
        x = x + g2 * rms_norm(f_lat.reshape(bsz, n_lat, d), norm_g[l, 3])
    return x
```

```python
import functools
import math

import jax
import jax.numpy as jnp
from jax import lax
from jax.experimental import pallas as pl
from jax.experimental.pallas import tpu as pltpu

F32 = jnp.float32
BF16 = jnp.bfloat16
U32 = jnp.uint32
I32 = jnp.int32

LANES = 128
MOD_ROWS = 8
TOP_K = 4
GRID_W = 64
ROPE_THETA = 10000.0
EPS = 1e-6
SWIGLU_LIMIT = 7.0
SWIGLU_ALPHA = 1.702
VMEM_LIMIT = 56 * 1024 * 1024

ROW_TILE = 256
MOE_BLOCK = 128


def _tile(n, *prefs):
    return next(t for t in prefs if n % t == 0)


def _params(*sem):
    return pltpu.CompilerParams(dimension_semantics=sem, vmem_limit_bytes=VMEM_LIMIT)


def _rms(v, g):
    return v * lax.rsqrt(jnp.mean(v * v, axis=-1, keepdims=True) + EPS) * g


def _pack_rows(v, out_ref, n_rows):
    d = v.shape[1]
    half = d // 2
    words_rows = half // LANES
    bits = pltpu.bitcast(v.astype(BF16).astype(F32), U32)
    w = (bits[:, half:] & jnp.uint32(0xFFFF0000)) | (bits[:, :half] >> 16)
    for s in range(words_rows):
        out_ref[pl.ds(s, n_rows, stride=words_rows), :] = w[:, s * LANES:(s + 1) * LANES]


def _unpack_rows(ref, n_rows, words_rows, lead=None):
    lo, hi = [], []
    for s in range(words_rows):
        idx = (pl.ds(s, n_rows, stride=words_rows), slice(None))
        w = ref[idx] if lead is None else ref[(lead,) + idx]
        lo.append(pltpu.bitcast(w << 16, F32))
        hi.append(pltpu.bitcast(w & jnp.uint32(0xFFFF0000), F32))
    return jnp.concatenate(lo, axis=1), jnp.concatenate(hi, axis=1)


def _mod_kernel(a_ref, w_ref, b_ref, o_ref):
    a = a_ref[...]
    a = (a * jax.nn.sigmoid(a)).astype(BF16)
    o_ref[...] = jnp.dot(a, w_ref[...].astype(BF16), preferred_element_type=F32) + b_ref[...]


def _modulation(act, w_mod, b_mod):
    depth, d, n = w_mod.shape
    tn = 512
    return pl.pallas_call(
        _mod_kernel,
        grid=(depth, n // tn),
        in_specs=[
            pl.BlockSpec((MOD_ROWS, d), lambda l, j: (0, 0)),
            pl.BlockSpec((None, d, tn), lambda l, j: (l, 0, j)),
            pl.BlockSpec((None, 1, tn), lambda l, j: (l, 0, j)),
        ],
        out_specs=pl.BlockSpec((None, MOD_ROWS, tn), lambda l, j: (l, 0, j)),
        out_shape=jax.ShapeDtypeStruct((depth, MOD_ROWS, n), F32),
        compiler_params=_params("arbitrary", "arbitrary"),
        name="modulation",
    )(act, w_mod, b_mod.reshape(depth, 1, n))


class _Mod:
    def __init__(self, table, layer, n_lat_rows, seq, batch, tile):
        self.table = table
        self.layer = layer
        self.lat_tiles = n_lat_rows // tile
        self.seq_tiles = seq // tile
        self.batch = batch
        self.d = table.shape[-1]

    def spec(self, which):
        base = self.layer * MOD_ROWS * 6 + which

        def index(i):
            row = jnp.where(i < self.lat_tiles, i // self.seq_tiles, self.batch)
            return (base + row * 6, 0, 0)

        return pl.BlockSpec((None, 1, self.d), index)


def _norm_mod_kernel(x_ref, g_ref, sh_ref, sc_ref, o_ref):
    y = _rms(x_ref[...], g_ref[...])
    o_ref[...] = (y * (1.0 + sc_ref[...]) + sh_ref[...]).astype(o_ref.dtype)


def _norm_mod(x, g, mod, n_rows):
    d = x.shape[1]
    row = pl.BlockSpec((ROW_TILE, d), lambda i: (i, 0))
    return pl.pallas_call(
        _norm_mod_kernel,
        grid=(n_rows // ROW_TILE,),
        in_specs=[row, pl.BlockSpec((1, d), lambda i: (0, 0)), mod.spec(0), mod.spec(1)],
        out_specs=row,
        out_shape=jax.ShapeDtypeStruct((n_rows, d), BF16),
        compiler_params=_params("arbitrary"),
        name="norm_mod",
    )(x, g.reshape(1, d), mod.table, mod.table)


def _matmul_kernel(*refs):
    n_pairs = (len(refs) - 1) // 2
    o_ref = refs[-1]
    acc = None
    for p in range(n_pairs):
        part = jnp.dot(refs[p][...], refs[n_pairs + p][...], preferred_element_type=F32)
        acc = part if acc is None else acc + part
    o_ref[...] = acc.astype(o_ref.dtype)


def _matmul(a_list, w_list, n_rows, tm, tn, name):
    n = w_list[0].shape[1]
    in_specs = [pl.BlockSpec((tm, a.shape[1]), lambda i, j: (i, 0)) for a in a_list]
    in_specs += [pl.BlockSpec((w.shape[0], tn), lambda i, j: (0, j)) for w in w_list]
    return pl.pallas_call(
        _matmul_kernel,
        grid=(n_rows // tm, n // tn),
        in_specs=in_specs,
        out_specs=pl.BlockSpec((tm, tn), lambda i, j: (i, j)),
        out_shape=jax.ShapeDtypeStruct((n_rows, n), BF16),
        compiler_params=_params("arbitrary", "arbitrary"),
        name=name,
    )(*a_list, *w_list)


def _gmlp_kernel(z_ref, lng_ref, lnb_ref, ws_ref, bs_ref, o_ref, *, heads, head_dim, chunk, chunks):
    width = heads * head_dim

    def one_chunk(ci, carry):
        rows = pl.ds(pl.multiple_of(ci * chunk, chunk), chunk)
        for h in range(heads):
            cu = slice(h * head_dim, (h + 1) * head_dim)
            cv = slice(width + h * head_dim, width + (h + 1) * head_dim)
            u = jax.nn.gelu(z_ref[rows, cu].astype(F32))
            v = jax.nn.gelu(z_ref[rows, cv].astype(F32))
            vc = v - jnp.mean(v, axis=-1, keepdims=True)
            v = vc * lax.rsqrt(jnp.mean(vc * vc, axis=-1, keepdims=True) + EPS)
            v = v * lng_ref[:, cu] + lnb_ref[:, cu]
            mixed = jnp.dot(ws_ref[h], v.astype(BF16), preferred_element_type=F32) + bs_ref[:, h:h + 1]
            o_ref[rows, cu] = (u * mixed).astype(o_ref.dtype)
        return carry

    lax.fori_loop(0, chunks, one_chunk, 0)


def _gmlp(z, ln_g, ln_b, w_s, b_s, n_rows):
    heads, chunk, _ = w_s.shape
    width = ln_g.shape[0]
    chunks = 2
    tile = chunk * chunks
    kern = functools.partial(_gmlp_kernel, heads=heads, head_dim=width // heads, chunk=chunk, chunks=chunks)
    return pl.pallas_call(
        kern,
        grid=(n_rows // tile,),
        in_specs=[
            pl.BlockSpec((tile, 2 * width), lambda i: (i, 0)),
            pl.BlockSpec((1, width), lambda i: (0, 0)),
            pl.BlockSpec((1, width), lambda i: (0, 0)),
            pl.BlockSpec((heads, chunk, chunk), lambda i: (0, 0, 0)),
            pl.BlockSpec((chunk, heads), lambda i: (0, 0)),
        ],
        out_specs=pl.BlockSpec((tile, width), lambda i: (i, 0)),
        out_shape=jax.ShapeDtypeStruct((n_rows, width), BF16),
        compiler_params=_params("arbitrary"),
        name="gmlp",
    )(z, ln_g.reshape(1, width), ln_b.reshape(1, width), w_s.astype(BF16), b_s.T)


def _rope_tables(seq, head_dim):
    rows = seq // GRID_W
    row = jnp.repeat(jnp.arange(rows, dtype=F32), GRID_W)
    col = jnp.tile(jnp.arange(GRID_W, dtype=F32), rows)
    axis_dim = head_dim // 2
    inv_freq = jnp.power(ROPE_THETA, -jnp.arange(0, axis_dim, 2, dtype=F32) / axis_dim)
    ang_r = row[:, None] * inv_freq
    ang_c = col[:, None] * inv_freq
    ang = jnp.concatenate([ang_r, ang_r, ang_c, ang_c], axis=-1)
    cos, sin = jnp.cos(ang), jnp.sin(ang)
    quarter = (jnp.arange(head_dim) // (head_dim // 4)) % 2
    sin_up = jnp.where(quarter == 0, -sin, 0.0)
    sin_dn = jnp.where(quarter == 1, sin, 0.0)
    return cos, sin_up, sin_dn


def _rope_kernel(z_ref, cos_ref, up_ref, dn_ref, o_ref, *, head_dim, groups):
    cos, up, dn = cos_ref[...], up_ref[...], dn_ref[...]
    q4 = head_dim // 4
    for gidx in range(groups):
        cols = slice(gidx * head_dim, (gidx + 1) * head_dim)
        t = z_ref[:, cols].astype(F32)
        r = t * cos + pltpu.roll(t, head_dim - q4, 1) * up + pltpu.roll(t, q4, 1) * dn
        o_ref[:, cols] = r.astype(o_ref.dtype)


def _rope(z, tables, n_rows, seq, col_off, width, head_dim):
    tr = min(512, seq)
    seq_tiles = seq // tr
    tab = pl.BlockSpec((tr, head_dim), lambda i: (i % seq_tiles, 0))
    kern = functools.partial(_rope_kernel, head_dim=head_dim, groups=width // head_dim)
    return pl.pallas_call(
        kern,
        grid=(n_rows // tr,),
        in_specs=[pl.BlockSpec((tr, width), lambda i: (i, col_off // width)), tab, tab, tab],
        out_specs=pl.BlockSpec((tr, width), lambda i: (i, 0)),
        out_shape=jax.ShapeDtypeStruct((n_rows, width), BF16),
        compiler_params=_params("arbitrary"),
        name="rope",
    )(z, *tables)


def _attn_kernel(*refs, n_seg, lam_init, head_dim):
    lq_ref, lk_ref, g_ref, q0_ref, q1_ref = refs[:5]
    seg = refs[5:5 + 3 * n_seg]
    o_ref = refs[-1]
    lam = (jnp.exp(jnp.sum(lq_ref[0:1, :] * lk_ref[0:1, :], axis=-1, keepdims=True))
           - jnp.exp(jnp.sum(lq_ref[1:2, :] * lk_ref[1:2, :], axis=-1, keepdims=True)) + lam_init)
    scale = head_dim ** -0.5
    nt = (((1,), (1,)), ((), ()))
    a = [None] * n_seg
    for c, q_ref in enumerate((q0_ref, q1_ref)):
        q = q_ref[...]
        s = [lax.dot_general(q, seg[3 * i + c][...], nt, preferred_element_type=F32) * scale for i in range(n_seg)]
        m = functools.reduce(jnp.maximum, [jnp.max(si, axis=-1, keepdims=True) for si in s])
        p = [jnp.exp(si - m) for si in s]
        denom = functools.reduce(jnp.add, [jnp.sum(pi, axis=-1, keepdims=True) for pi in p])
        if c == 0:
            a = [pi / denom for pi in p]
        else:
            a = [ai - lam * (pi / denom) for ai, pi in zip(a, p)]
    o = functools.reduce(jnp.add, [jnp.dot(a[i].astype(BF16), seg[3 * i + 2][...], preferred_element_type=F32)
                                   for i in range(n_seg)])
    o_ref[...] = (_rms(o, g_ref[...]) * (1.0 - lam_init)).astype(o_ref.dtype)


def _attention(q_src, q_col_off, q_row_blocks, segs, lam_q, lam_k, subln_g, layer, lam_init,
               batch, q_len, heads, head_dim, v_dim, tq):
    qt = q_len // tq
    qc = q_col_off // head_dim

    def q_spec(c):
        return pl.BlockSpec((tq, head_dim), lambda b, h, i: (q_row_blocks + b * qt + i, qc + 2 * h + c))

    in_specs = [
        pl.BlockSpec((None, 2, head_dim), lambda b, h, i: (layer, 0, 0)),
        pl.BlockSpec((None, 2, head_dim), lambda b, h, i: (layer, 0, 0)),
        pl.BlockSpec((1, v_dim), lambda b, h, i: (0, 0)),
        q_spec(0), q_spec(1),
    ]
    args = [lam_q, lam_k, subln_g.reshape(1, v_dim), q_src, q_src]
    for (k_arr, k_off, v_arr, v_off, seg_len, row_off) in segs:
        kc, vc = k_off // head_dim, v_off // v_dim
        for c in range(2):
            in_specs.append(pl.BlockSpec(
                (seg_len, head_dim), lambda b, h, i, kc=kc, c=c, row_off=row_off: (row_off + b, kc + 2 * h + c)))
            args.append(k_arr)
        in_specs.append(pl.BlockSpec((seg_len, v_dim), lambda b, h, i, vc=vc, row_off=row_off: (row_off + b, vc + h)))
        args.append(v_arr)
    kern = functools.partial(_attn_kernel, n_seg=len(segs), lam_init=lam_init, head_dim=head_dim)
    return pl.pallas_call(
        kern,
        grid=(batch, heads, qt),
        in_specs=in_specs,
        out_specs=pl.BlockSpec((tq, v_dim), lambda b, h, i: (b * qt + i, h)),
        out_shape=jax.ShapeDtypeStruct((batch * q_len, heads * v_dim), BF16),
        compiler_params=_params("arbitrary", "arbitrary", "arbitrary"),
        name="diff_attention",
    )(*args)


def _post_mix_kernel(x_ref, m_ref, gate_ref, gm_ref, g2_ref, sh_ref, sc_ref, rw_ref, rb_ref,
                     xo_ref, hp_ref, lg_ref):
    xn = x_ref[...] + gate_ref[...] * _rms(m_ref[...].astype(F32), gm_ref[...])
    xo_ref[...] = xn
    h2 = _rms(xn, g2_ref[...]) * (1.0 + sc_ref[...]) + sh_ref[...]
    _pack_rows(h2, hp_ref, ROW_TILE)
    nt = (((1,), (1,)), ((), ()))
    lg_ref[...] = lax.dot_general(rw_ref[...], h2.astype(BF16), nt, preferred_element_type=F32) + rb_ref[...]


def _post_mix(x, mixo, mod, g_mix, g_ffn, router_w, router_b, n_rows):
    d = x.shape[1]
    e = router_w.shape[1]
    wr = d // 2 // LANES
    row = pl.BlockSpec((ROW_TILE, d), lambda i: (i, 0))
    vec = pl.BlockSpec((1, d), lambda i: (0, 0))
    return pl.pallas_call(
        _post_mix_kernel,
        grid=(n_rows // ROW_TILE,),
        in_specs=[row, row, mod.spec(2), vec, vec, mod.spec(3), mod.spec(4),
                  pl.BlockSpec((e, d), lambda i: (0, 0)), pl.BlockSpec((e, 1), lambda i: (0, 0))],
        out_specs=[row, pl.BlockSpec((ROW_TILE * wr, LANES), lambda i: (i, 0)),
                   pl.BlockSpec((e, ROW_TILE), lambda i: (0, i))],
        out_shape=[jax.ShapeDtypeStruct((n_rows, d), F32),
                   jax.ShapeDtypeStruct((n_rows * wr, LANES), U32),
                   jax.ShapeDtypeStruct((e, n_rows), F32)],
        compiler_params=_params("arbitrary"),
        name="post_mix",
    )(x, mixo, mod.table, g_mix.reshape(1, d), g_ffn.reshape(1, d), mod.table, mod.table,
      router_w.T.astype(BF16), router_b.reshape(e, 1))


def _router_kernel(lg_ref, idx_ref, pos_ref, gate_ref, cnt_ref, carry_ref):
    i = pl.program_id(0)

    @pl.when(i == 0)
    def _():
        carry_ref[...] = jnp.zeros_like(carry_ref)

    work = lg_ref[...]
    e, tr = work.shape
    e_iota = lax.broadcasted_iota(I32, (e, tr), 0)
    vals, idxs = [], []
    chosen = jnp.zeros((e, tr), dtype=jnp.bool_)
    for _ in range(TOP_K):
        m = jnp.max(work, axis=0, keepdims=True)
        idx = jnp.min(jnp.where(work == m, e_iota, e), axis=0, keepdims=True)
        sel = e_iota == idx
        vals.append(m)
        idxs.append(idx)
        chosen = jnp.logical_or(chosen, sel)
        work = jnp.where(sel, -jnp.inf, work)
    ex = [jnp.exp(v - vals[0]) for v in vals]
    denom = functools.reduce(jnp.add, ex)
    onehot = jnp.where(chosen, 1.0, 0.0)
    before = (lax.broadcasted_iota(I32, (tr, tr), 0) < lax.broadcasted_iota(I32, (tr, tr), 1))
    rank = jnp.dot(onehot.astype(BF16), jnp.where(before, 1.0, 0.0).astype(BF16), preferred_element_type=F32)
    rank = rank + carry_ref[:, 0:1]
    for k in range(TOP_K):
        idx_ref[k:k + 1, :] = idxs[k]
        gate_ref[k:k + 1, :] = ex[k] / denom
        pos_ref[k:k + 1, :] = jnp.sum(jnp.where(e_iota == idxs[k], rank, 0.0), axis=0, keepdims=True).astype(I32)
    carry_ref[...] = carry_ref[...] + jnp.sum(onehot, axis=1, keepdims=True)
    cnt_ref[...] = carry_ref[...]


def _router(logits_t):
    e, n_tok = logits_t.shape
    tr = 512
    out = pl.BlockSpec((TOP_K, tr), lambda i: (0, i))
    return pl.pallas_call(
        _router_kernel,
        grid=(n_tok // tr,),
        in_specs=[pl.BlockSpec((e, tr), lambda i: (0, i))],
        out_specs=[out, out, out, pl.BlockSpec((e, LANES), lambda i: (0, 0))],
        out_shape=[jax.ShapeDtypeStruct((TOP_K, n_tok), I32), jax.ShapeDtypeStruct((TOP_K, n_tok), I32),
                   jax.ShapeDtypeStruct((TOP_K, n_tok), F32), jax.ShapeDtypeStruct((e, LANES), F32)],
        scratch_shapes=[pltpu.VMEM((e, LANES), F32)],
        compiler_params=_params("arbitrary"),
        name="router",
    )(logits_t)


def _dispatch_kernel(start_ref, cnt_ref, pad_ref, dest_ref, hp_ref, xs_ref, zero_ref, sem, *, tile, wr, n_exp):
    i = pl.program_id(0)

    def row_copy(src, dst_row):
        return pltpu.make_async_copy(src, xs_ref.at[pl.ds(pl.multiple_of(dst_row * wr, wr), wr)], sem)

    @pl.when(i == 0)
    def _():
        zero_ref[...] = jnp.zeros_like(zero_ref)
        for phase in ("start", "wait"):
            for ex in range(n_exp):
                def pad_row(r, c, ex=ex, phase=phase):
                    cp = row_copy(zero_ref, start_ref[ex] + r)
                    cp.start() if phase == "start" else cp.wait()
                    return c
                lax.fori_loop(cnt_ref[ex], pad_ref[ex], pad_row, 0)

    def tok_start(t, c):
        src = hp_ref.at[pl.ds(pl.multiple_of((i * tile + t) * wr, wr), wr)]
        for k in range(TOP_K):
            row_copy(src, dest_ref[t * TOP_K + k]).start()
        return c

    def tok_wait(t, c):
        src = hp_ref.at[pl.ds(pl.multiple_of((i * tile + t) * wr, wr), wr)]
        for k in range(TOP_K):
            row_copy(src, dest_ref[t * TOP_K + k]).wait()
        return c

    lax.fori_loop(0, tile, tok_start, 0)
    lax.fori_loop(0, tile, tok_wait, 0)


def _dispatch(hp, dest_flat, pad_start, counts, padded, n_tok, n_rows, wr):
    tile = 512
    n_exp = pad_start.shape[0]
    kern = functools.partial(_dispatch_kernel, tile=tile, wr=wr, n_exp=n_exp)
    return pl.pallas_call(
        kern,
        grid_spec=pltpu.PrefetchScalarGridSpec(
            num_scalar_prefetch=3,
            grid=(n_tok // tile,),
            in_specs=[pl.BlockSpec((tile * TOP_K,), lambda i, *_: (i,), memory_space=pltpu.SMEM),
                      pl.BlockSpec(memory_space=pl.ANY)],
            out_specs=pl.BlockSpec(memory_space=pl.ANY),
            scratch_shapes=[pltpu.VMEM((wr, LANES), U32), pltpu.SemaphoreType.DMA(())],
        ),
        out_shape=jax.ShapeDtypeStruct((n_rows * wr, LANES), U32),
        compiler_params=_params("arbitrary"),
        name="moe_dispatch",
    )(pad_start, counts, padded, dest_flat, hp)


def _expert_kernel(blk_e_ref, used_ref, xs_ref, wg_ref, wu_ref, bg_ref, bu_ref, wd_ref, bd_ref, y_ref, *, wr):
    j = pl.program_id(0)

    @pl.when(j < used_ref[0])
    def _():
        lo, hi = _unpack_rows(xs_ref, MOE_BLOCK, wr)
        half = lo.shape[1]
        lo, hi = lo.astype(BF16), hi.astype(BF16)

        def proj(w_ref, b_ref):
            return (jnp.dot(lo, w_ref[:half, :], preferred_element_type=F32)
                    + jnp.dot(hi, w_ref[half:, :], preferred_element_type=F32) + b_ref[...])

        gate = jnp.minimum(proj(wg_ref, bg_ref), SWIGLU_LIMIT)
        up = jnp.clip(proj(wu_ref, bu_ref), -SWIGLU_LIMIT, SWIGLU_LIMIT)
        act = (up + 1.0) * (gate * jax.nn.sigmoid(SWIGLU_ALPHA * gate))
        out = jnp.dot(act.astype(BF16), wd_ref[...], preferred_element_type=F32) + bd_ref[...]
        _pack_rows(out, y_ref, MOE_BLOCK)


def _experts(xs, blk_e, n_used, w_gate, w_up, b_gate, b_up, w_down, b_down, n_blk, wr):
    n_exp, d, de = w_gate.shape
    rows = pl.BlockSpec((MOE_BLOCK * wr, LANES), lambda j, be, nu: (jnp.minimum(j, nu[0] - 1), 0))
    kern = functools.partial(_expert_kernel, wr=wr)
    return pl.pallas_call(
        kern,
        grid_spec=pltpu.PrefetchScalarGridSpec(
            num_scalar_prefetch=2,
            grid=(n_blk,),
            in_specs=[rows,
                      pl.BlockSpec((None, d, de), lambda j, be, nu: (be[j], 0, 0)),
                      pl.BlockSpec((None, d, de), lambda j, be, nu: (be[j], 0, 0)),
                      pl.BlockSpec((None, 1, de), lambda j, be, nu: (be[j], 0, 0)),
                      pl.BlockSpec((None, 1, de), lambda j, be, nu: (be[j], 0, 0)),
                      pl.BlockSpec((None, de, d), lambda j, be, nu: (be[j], 0, 0)),
                      pl.BlockSpec((None, 1, d), lambda j, be, nu: (be[j], 0, 0))],
            out_specs=rows,
        ),
        out_shape=jax.ShapeDtypeStruct((n_blk * MOE_BLOCK * wr, LANES), U32),
        compiler_params=_params("arbitrary"),
        name="moe_experts",
    )(blk_e, n_used, xs, w_gate, w_up, b_gate, b_up, w_down, b_down)


def _combine_kernel(dest_ref, x_ref, gates_ref, g2_ref, gn_ref, y_ref, xo_ref, buf_ref, sem, *, tile, wr):
    def copies(t, k):
        return pltpu.make_async_copy(
            y_ref.at[pl.ds(pl.multiple_of(dest_ref[t * TOP_K + k] * wr, wr), wr)],
            buf_ref.at[k, pl.ds(pl.multiple_of(t * wr, wr), wr)], sem)

    def tok_start(t, c):
        for k in range(TOP_K):
            copies(t, k).start()
        return c

    def tok_wait(t, c):
        for k in range(TOP_K):
            copies(t, k).wait()
        return c

    lax.fori_loop(0, tile, tok_start, 0)
    lax.fori_loop(0, tile, tok_wait, 0)
    f_lo = f_hi = None
    for k in range(TOP_K):
        lo, hi = _unpack_rows(buf_ref, tile, wr, lead=k)
        gk = gates_ref[:, k:k + 1]
        f_lo = lo * gk if f_lo is None else f_lo + lo * gk
        f_hi = hi * gk if f_hi is None else f_hi + hi * gk
    f = jnp.concatenate([f_lo, f_hi], axis=1)
    xo_ref[...] = x_ref[...] + g2_ref[...] * _rms(f, gn_ref[...])


def _combine(x, y, dest_flat, gates, mod, g_out, n_rows, wr):
    d = x.shape[1]
    tile = ROW_TILE
    row = pl.BlockSpec((tile, d), lambda i: (i, 0))
    kern = functools.partial(_combine_kernel, tile=tile, wr=wr)
    return pl.pallas_call(
        kern,
        grid=(n_rows // tile,),
        in_specs=[pl.BlockSpec((tile * TOP_K,), lambda i: (i,), memory_space=pltpu.SMEM),
                  row, pl.BlockSpec((tile, TOP_K), lambda i: (i, 0)), mod.spec(5),
                  pl.BlockSpec((1, d), lambda i: (0, 0)), pl.BlockSpec(memory_space=pl.ANY)],
        out_specs=row,
        out_shape=jax.ShapeDtypeStruct((n_rows, d), F32),
        scratch_shapes=[pltpu.VMEM((TOP_K, tile * wr, LANES), U32), pltpu.SemaphoreType.DMA(())],
        compiler_params=_params("arbitrary"),
        name="moe_combine",
    )(dest_flat, x, gates, mod.table, g_out.reshape(1, d), y)


def _moe(x_new, hp, logits_t, mod, g_out, w_gate, w_up, b_gate, b_up, w_down, b_down, n_rows, wr):
    n_exp = w_gate.shape[0]
    idx, pos, gates, counts = _router(logits_t)
    counts = counts[:, 0].astype(I32)
    padded = (counts + MOE_BLOCK - 1) // MOE_BLOCK * MOE_BLOCK
    pad_end = jnp.cumsum(padded)
    pad_start = pad_end - padded
    n_blk = -(-(n_rows * TOP_K) // MOE_BLOCK) + n_exp
    blk_first = jnp.arange(n_blk, dtype=I32) * MOE_BLOCK
    blk_e = jnp.minimum(jnp.sum(blk_first[:, None] >= pad_end[None, :], axis=1), n_exp - 1).astype(I32)
    n_used = (pad_end[-1:] // MOE_BLOCK).astype(I32)
    dest_flat = (pad_start[idx] + pos).T.reshape(-1)
    xs = _dispatch(hp, dest_flat, pad_start.astype(I32), counts, padded.astype(I32),
                   n_rows, n_blk * MOE_BLOCK, wr)
    y = _experts(xs, blk_e, n_used, w_gate, w_up, b_gate, b_up, w_down, b_down, n_blk, wr)
    return _combine(x_new, y, dest_flat, gates.T, mod, g_out, n_rows, wr)


def kernel(x, c, ctx, c_ctx, w_mod, b_mod, norm_g, w_in, gmlp_ln_g, gmlp_ln_b, w_spatial, b_spatial,
           lambda_q, lambda_k, subln_g, w_out, router_w, router_b, w_gate_up, b_gate_up, w_down, b_down):
    batch, seq, d = x.shape
    n_ctx = ctx.shape[1]
    depth = w_in.shape[0]
    in_cols = w_in.shape[2]
    a_width = gmlp_ln_g.shape[1]
    head_dim = lambda_q.shape[2]
    v_dim = subln_g.shape[1]
    heads = (in_cols - 2 * a_width) // (4 * head_dim + v_dim)
    qk_width = heads * 2 * head_dim
    q_off = 2 * a_width
    k_off = q_off + qk_width
    val_off = k_off + qk_width
    n_exp = router_w.shape[2]
    de = w_down.shape[2]
    wr = d // 2 // LANES
    n_lat, n_cx = batch * seq, batch * n_ctx
    n_all = n_lat + n_cx
    assert batch < MOD_ROWS and seq % ROW_TILE == 0 and n_ctx % ROW_TILE == 0
    assert seq % n_ctx == 0 and d % (2 * LANES) == 0

    xa = jnp.concatenate([x.reshape(n_lat, d), ctx.reshape(n_cx, d)], axis=0)
    act = jnp.zeros((MOD_ROWS, d), F32).at[:batch].set(c).at[batch].set(c_ctx)
    mod_table = _modulation(act, w_mod, b_mod).reshape(depth * MOD_ROWS * 6, 1, d)
    tables = _rope_tables(seq, head_dim)
    tm = _tile(n_all, 1024, ROW_TILE)

    for l in range(depth):
        last = l == depth - 1
        lam_init = 0.8 - 0.6 * math.exp(-0.3 * l)
        mod = _Mod(mod_table, l, n_lat, seq, batch, ROW_TILE)
        n_act = n_lat if last else n_all

        h = _norm_mod(xa, norm_g[l, 0], mod, n_all)
        z = _matmul([h], [w_in[l].astype(BF16)], n_all, tm, _tile(in_cols, 1024, 512, 256), "in_proj")

        a_mix = _gmlp(z, gmlp_ln_g[l], gmlp_ln_b[l], w_spatial[l], b_spatial[l], n_act)
        qk = _rope(z, tables, n_lat, seq, q_off, 2 * qk_width, head_dim)
        lat_seg = (qk, qk_width, z, val_off, seq, 0)
        ctx_seg = (z, k_off, z, val_off, n_ctx, n_lat // n_ctx)
        tq = min(256, seq)
        b_mix = _attention(qk, 0, 0, [ctx_seg, lat_seg], lambda_q, lambda_k, subln_g[l], l, lam_init,
                           batch, seq, heads, head_dim, v_dim, tq)
        if not last:
            tqc = min(256, n_ctx)
            b_ctx = _attention(z, q_off, n_lat // tqc, [ctx_seg], lambda_q, lambda_k, subln_g[l], l, lam_init,
                               batch, n_ctx, heads, head_dim, v_dim, tqc)
            b_mix = jnp.concatenate([b_mix, b_ctx], axis=0)
        w_o = w_out[l].astype(BF16)
        mixo = _matmul([a_mix, b_mix], [w_o[:a_width], w_o[a_width:]], n_act,
                       _tile(n_act, 1024, ROW_TILE), _tile(d, 512, 256), "out_proj")

        x_new, hp, logits_t = _post_mix(xa, mixo, mod, norm_g[l, 1], norm_g[l, 2], router_w[l], router_b[l], n_act)
        w_gu = w_gate_up[l].astype(BF16)
        xa = _moe(x_new, hp, logits_t, mod, norm_g[l, 3], w_gu[:, :, 0::2], w_gu[:, :, 1::2],
                  b_gate_up[l][:, None, 0::2], b_gate_up[l][:, None, 1::2],
                  w_down[l].astype(BF16), b_down[l][:, None, :], n_act, wr)
    return xa[:n_lat].reshape(batch, seq, d)
```

```python
import functools
import math

import jax
import jax.numpy as jnp
from jax import lax
from jax.experimental import pallas as pl
from jax.experimental.pallas import tpu as pltpu

F32 = jnp.float32
BF16 = jnp.bfloat16
U32 = jnp.uint32
I32 = jnp.int32

LANES = 128
MOD_ROWS = 8
TOP_K = 4
GRID_W = 64
ROPE_THETA = 10000.0
EPS = 1e-6
SWIGLU_LIMIT = 7.0
SWIGLU_ALPHA = 1.702
VMEM_LIMIT = 56 * 1024 * 1024

ROW_TILE = 256
MOE_BLOCK = 256
PAIR_GROUP = 256


def _tile(n, *prefs):
    return next(t for t in prefs if n % t == 0)


def _params(*sem):
    return pltpu.CompilerParams(dimension_semantics=sem, vmem_limit_bytes=VMEM_LIMIT)


def _rms(v, g):
    return v * lax.rsqrt(jnp.mean(v * v, axis=-1, keepdims=True) + EPS) * g


def _pack_rows(v, out_ref, n_rows):
    d = v.shape[1]
    half = d // 2
    words_rows = half // LANES
    bits = pltpu.bitcast(v.astype(BF16).astype(F32), U32)
    w = (bits[:, half:] & jnp.uint32(0xFFFF0000)) | (bits[:, :half] >> 16)
    for s in range(words_rows):
        out_ref[pl.ds(s, n_rows, stride=words_rows), :] = w[:, s * LANES:(s + 1) * LANES]


def _unpack_rows(ref, n_rows, words_rows, lead=None):
    lo, hi = [], []
    for s in range(words_rows):
        idx = (pl.ds(s, n_rows, stride=words_rows), slice(None))
        w = ref[idx] if lead is None else ref[(lead,) + idx]
        lo.append(pltpu.bitcast(w << 16, F32))
        hi.append(pltpu.bitcast(w & jnp.uint32(0xFFFF0000), F32))
    return jnp.concatenate(lo, axis=1), jnp.concatenate(hi, axis=1)


def _mod_kernel(a_ref, w_ref, b_ref, o_ref):
    a = a_ref[...]
    a = (a * jax.nn.sigmoid(a)).astype(BF16)
    o_ref[...] = jnp.dot(a, w_ref[...].astype(BF16), preferred_element_type=F32) + b_ref[...]


def _modulation(act, w_mod, b_mod):
    depth, d, n = w_mod.shape
    tn = 512
    return pl.pallas_call(
        _mod_kernel,
        grid=(depth, n // tn),
        in_specs=[
            pl.BlockSpec((MOD_ROWS, d), lambda l, j: (0, 0)),
            pl.BlockSpec((None, d, tn), lambda l, j: (l, 0, j)),
            pl.BlockSpec((None, 1, tn), lambda l, j: (l, 0, j)),
        ],
        out_specs=pl.BlockSpec((None, MOD_ROWS, tn), lambda l, j: (l, 0, j)),
        out_shape=jax.ShapeDtypeStruct((depth, MOD_ROWS, n), F32),
        compiler_params=_params("arbitrary", "arbitrary"),
        name="modulation",
    )(act, w_mod, b_mod.reshape(depth, 1, n))


class _Mod:
    def __init__(self, table, layer, n_lat_rows, seq, batch, tile):
        self.table = table
        self.layer = layer
        self.lat_tiles = n_lat_rows // tile
        self.seq_tiles = seq // tile
        self.batch = batch
        self.d = table.shape[-1]

    def spec(self, which):
        base = self.layer * MOD_ROWS * 6 + which

        def index(i):
            row = jnp.where(i < self.lat_tiles, i // self.seq_tiles, self.batch)
            return (base + row * 6, 0, 0)

        return pl.BlockSpec((None, 1, self.d), index)


def _norm_mod_kernel(x_ref, g_ref, sh_ref, sc_ref, o_ref):
    y = _rms(x_ref[...], g_ref[...])
    o_ref[...] = (y * (1.0 + sc_ref[...]) + sh_ref[...]).astype(o_ref.dtype)


def _norm_mod(x, g, mod, n_rows):
    d = x.shape[1]
    row = pl.BlockSpec((ROW_TILE, d), lambda i: (i, 0))
    return pl.pallas_call(
        _norm_mod_kernel,
        grid=(n_rows // ROW_TILE,),
        in_specs=[row, pl.BlockSpec((1, d), lambda i: (0, 0)), mod.spec(0), mod.spec(1)],
        out_specs=row,
        out_shape=jax.ShapeDtypeStruct((n_rows, d), BF16),
        compiler_params=_params("arbitrary"),
        name="norm_mod",
    )(x, g.reshape(1, d), mod.table, mod.table)


def _matmul_kernel(*refs):
    n_pairs = (len(refs) - 1) // 2
    o_ref = refs[-1]
    acc = None
    for p in range(n_pairs):
        part = jnp.dot(refs[p][...], refs[n_pairs + p][...], preferred_element_type=F32)
        acc = part if acc is None else acc + part
    o_ref[...] = acc.astype(o_ref.dtype)


def _matmul(a_list, w, layer, n_rows, tm, tn, name):
    _, k_total, n = w.shape
    kb = k_total // len(a_list)
    assert all(a.shape[1] == kb for a in a_list)
    in_specs = [pl.BlockSpec((tm, kb), lambda i, j: (i, 0)) for _ in a_list]
    in_specs += [pl.BlockSpec((None, kb, tn), lambda i, j, p=p: (layer, p, j)) for p in range(len(a_list))]
    return pl.pallas_call(
        _matmul_kernel,
        grid=(n_rows // tm, n // tn),
        in_specs=in_specs,
        out_specs=pl.BlockSpec((tm, tn), lambda i, j: (i, j)),
        out_shape=jax.ShapeDtypeStruct((n_rows, n), BF16),
        compiler_params=_params("arbitrary", "arbitrary"),
        name=name,
    )(*a_list, *([w] * len(a_list)))


def _gmlp_kernel(z_ref, lng_ref, lnb_ref, ws_ref, bs_ref, o_ref, *, heads, head_dim, chunk, chunks):
    width = heads * head_dim

    def one_chunk(ci, carry):
        rows = pl.ds(pl.multiple_of(ci * chunk, chunk), chunk)
        for h in range(heads):
            cu = slice(h * head_dim, (h + 1) * head_dim)
            cv = slice(width + h * head_dim, width + (h + 1) * head_dim)
            u = jax.nn.gelu(z_ref[rows, cu].astype(F32))
            v = jax.nn.gelu(z_ref[rows, cv].astype(F32))
            vc = v - jnp.mean(v, axis=-1, keepdims=True)
            v = vc * lax.rsqrt(jnp.mean(vc * vc, axis=-1, keepdims=True) + EPS)
            v = v * lng_ref[:, cu] + lnb_ref[:, cu]
            mixed = jnp.dot(ws_ref[h], v.astype(BF16), preferred_element_type=F32) + bs_ref[:, h:h + 1]
            o_ref[rows, cu] = (u * mixed).astype(o_ref.dtype)
        return carry

    lax.fori_loop(0, chunks, one_chunk, 0)


def _gmlp(z, ln_g, ln_b, w_s, b_s, n_rows):
    heads, chunk, _ = w_s.shape
    width = ln_g.shape[0]
    chunks = 2
    tile = chunk * chunks
    kern = functools.partial(_gmlp_kernel, heads=heads, head_dim=width // heads, chunk=chunk, chunks=chunks)
    return pl.pallas_call(
        kern,
        grid=(n_rows // tile,),
        in_specs=[
            pl.BlockSpec((tile, 2 * width), lambda i: (i, 0)),
            pl.BlockSpec((1, width), lambda i: (0, 0)),
            pl.BlockSpec((1, width), lambda i: (0, 0)),
            pl.BlockSpec((heads, chunk, chunk), lambda i: (0, 0, 0)),
            pl.BlockSpec((chunk, heads), lambda i: (0, 0)),
        ],
        out_specs=pl.BlockSpec((tile, width), lambda i: (i, 0)),
        out_shape=jax.ShapeDtypeStruct((n_rows, width), BF16),
        compiler_params=_params("arbitrary"),
        name="gmlp",
    )(z, ln_g.reshape(1, width), ln_b.reshape(1, width), w_s.astype(BF16), b_s.T)


def _rope_tables(seq, head_dim):
    rows = seq // GRID_W
    row = jnp.repeat(jnp.arange(rows, dtype=F32), GRID_W)
    col = jnp.tile(jnp.arange(GRID_W, dtype=F32), rows)
    axis_dim = head_dim // 2
    inv_freq = jnp.power(ROPE_THETA, -jnp.arange(0, axis_dim, 2, dtype=F32) / axis_dim)
    ang_r = row[:, None] * inv_freq
    ang_c = col[:, None] * inv_freq
    ang = jnp.concatenate([ang_r, ang_r, ang_c, ang_c], axis=-1)
    cos, sin = jnp.cos(ang), jnp.sin(ang)
    quarter = (jnp.arange(head_dim) // (head_dim // 4)) % 2
    sin_up = jnp.where(quarter == 0, -sin, 0.0)
    sin_dn = jnp.where(quarter == 1, sin, 0.0)
    return cos, sin_up, sin_dn


def _rope_kernel(z_ref, cos_ref, up_ref, dn_ref, o_ref, *, head_dim, groups):
    cos, up, dn = cos_ref[...], up_ref[...], dn_ref[...]
    q4 = head_dim // 4
    q_scale = head_dim ** -0.5 * math.log2(math.e)
    for gidx in range(groups):
        cols = slice(gidx * head_dim, (gidx + 1) * head_dim)
        t = z_ref[:, cols].astype(F32)
        r = t * cos + pltpu.roll(t, head_dim - q4, 1) * up + pltpu.roll(t, q4, 1) * dn
        if gidx < groups // 2:
            r = r * q_scale
        o_ref[:, cols] = r.astype(o_ref.dtype)


def _rope(z, tables, n_rows, seq, col_off, width, head_dim):
    tr = min(512, seq)
    seq_tiles = seq // tr
    tab = pl.BlockSpec((tr, head_dim), lambda i: (i % seq_tiles, 0))
    kern = functools.partial(_rope_kernel, head_dim=head_dim, groups=width // head_dim)
    return pl.pallas_call(
        kern,
        grid=(n_rows // tr,),
        in_specs=[pl.BlockSpec((tr, width), lambda i: (i, col_off // width)), tab, tab, tab],
        out_specs=pl.BlockSpec((tr, width), lambda i: (i, 0)),
        out_shape=jax.ShapeDtypeStruct((n_rows, width), BF16),
        compiler_params=_params("arbitrary"),
        name="rope",
    )(z, *tables)


def _attn_kernel(*refs, n_seg, lam_init, head_dim):
    lq_ref, lk_ref, g_ref, q0_ref, q1_ref = refs[:5]
    seg = refs[5:5 + 3 * n_seg]
    o_ref = refs[-1]
    lam = (jnp.exp(jnp.sum(lq_ref[0:1, :] * lk_ref[0:1, :], axis=-1, keepdims=True))
           - jnp.exp(jnp.sum(lq_ref[1:2, :] * lk_ref[1:2, :], axis=-1, keepdims=True)) + lam_init)
    scale = head_dim ** -0.5
    nt = (((1,), (1,)), ((), ()))
    a = [None] * n_seg
    for c, q_ref in enumerate((q0_ref, q1_ref)):
        q = q_ref[...]
        s = [lax.dot_general(q, seg[3 * i + c][...], nt, preferred_element_type=F32) * scale for i in range(n_seg)]
        m = functools.reduce(jnp.maximum, [jnp.max(si, axis=-1, keepdims=True) for si in s])
        p = [jnp.exp(si - m) for si in s]
        denom = functools.reduce(jnp.add, [jnp.sum(pi, axis=-1, keepdims=True) for pi in p])
        if c == 0:
            a = [pi / denom for pi in p]
        else:
            a = [ai - lam * (pi / denom) for ai, pi in zip(a, p)]
    o = functools.reduce(jnp.add, [jnp.dot(a[i].astype(BF16), seg[3 * i + 2][...], preferred_element_type=F32)
                                   for i in range(n_seg)])
    o_ref[...] = (_rms(o, g_ref[...]) * (1.0 - lam_init)).astype(o_ref.dtype)


def _flash_kernel(lq_ref, lk_ref, g_ref, q0_ref, q1_ref, kc0_ref, kc1_ref, vc_ref, kl0_ref, kl1_ref, vl_ref,
                  o_ref, *, lam_init, tk):
    lam = (jnp.exp(jnp.sum(lq_ref[0:1, :] * lk_ref[0:1, :], axis=-1, keepdims=True))
           - jnp.exp(jnp.sum(lq_ref[1:2, :] * lk_ref[1:2, :], axis=-1, keepdims=True)) + lam_init)
    nt = (((1,), (1,)), ((), ()))
    chunks = kl0_ref.shape[0] // tk
    qs = (q0_ref[...], q1_ref[...])
    kls = (kl0_ref, kl1_ref)

    def first(q, kc_ref):
        s = lax.dot_general(q, kc_ref[...], nt, preferred_element_type=F32)
        m = jnp.max(s, axis=-1, keepdims=True)
        p = jnp.exp2(s - m)
        return m, jnp.sum(p, axis=-1, keepdims=True), jnp.dot(p.astype(BF16), vc_ref[...], preferred_element_type=F32)

    def chunk(j, carry):
        rows = pl.ds(pl.multiple_of(j * tk, tk), tk)
        v = vl_ref[rows, :]
        new = []
        for c in range(2):
            m, l, acc = carry[c]
            s = lax.dot_general(qs[c], kls[c][rows, :], nt, preferred_element_type=F32)
            m_new = jnp.maximum(m, jnp.max(s, axis=-1, keepdims=True))
            alpha = jnp.exp2(m - m_new)
            p = jnp.exp2(s - m_new)
            l = alpha * l + jnp.sum(p, axis=-1, keepdims=True)
            acc = alpha * acc + jnp.dot(p.astype(BF16), v, preferred_element_type=F32)
            new.append((m_new, l, acc))
        return tuple(new)

    carry = lax.fori_loop(0, chunks, chunk, (first(qs[0], kc0_ref), first(qs[1], kc1_ref)), unroll=True)
    outs = [acc * (1.0 / l) for (_, l, acc) in carry]
    o = outs[0] - lam * outs[1]
    o_ref[...] = (_rms(o, g_ref[...]) * (1.0 - lam_init)).astype(o_ref.dtype)


def _attention_latent(qk, z, k_off, val_off, lam_q, lam_k, subln_g, layer, lam_init,
                      batch, seq, n_ctx, heads, head_dim, v_dim, tq, tk):
    qt = seq // tq
    kq = heads * 2
    kc, vc = k_off // head_dim, val_off // v_dim
    ctx_blk = batch * seq // n_ctx
    lam_spec = pl.BlockSpec((None, 2, head_dim), lambda b, h, i: (layer, 0, 0))
    in_specs = [
        lam_spec, lam_spec, pl.BlockSpec((1, v_dim), lambda b, h, i: (0, 0)),
        pl.BlockSpec((tq, head_dim), lambda b, h, i: (b * qt + i, 2 * h)),
        pl.BlockSpec((tq, head_dim), lambda b, h, i: (b * qt + i, 2 * h + 1)),
        pl.BlockSpec((n_ctx, head_dim), lambda b, h, i: (ctx_blk + b, kc + 2 * h)),
        pl.BlockSpec((n_ctx, head_dim), lambda b, h, i: (ctx_blk + b, kc + 2 * h + 1)),
        pl.BlockSpec((n_ctx, v_dim), lambda b, h, i: (ctx_blk + b, vc + h)),
        pl.BlockSpec((seq, head_dim), lambda b, h, i: (b, kq + 2 * h)),
        pl.BlockSpec((seq, head_dim), lambda b, h, i: (b, kq + 2 * h + 1)),
        pl.BlockSpec((seq, v_dim), lambda b, h, i: (b, vc + h)),
    ]
    kern = functools.partial(_flash_kernel, lam_init=lam_init, tk=tk)
    return pl.pallas_call(
        kern,
        grid=(batch, heads, qt),
        in_specs=in_specs,
        out_specs=pl.BlockSpec((tq, v_dim), lambda b, h, i: (b * qt + i, h)),
        out_shape=jax.ShapeDtypeStruct((batch * seq, heads * v_dim), BF16),
        compiler_params=_params("arbitrary", "arbitrary", "arbitrary"),
        name="flash_diff_attention",
    )(lam_q, lam_k, subln_g.reshape(1, v_dim), qk, qk, z, z, z, qk, qk, z)


def _attention(q_src, q_col_off, q_row_blocks, segs, lam_q, lam_k, subln_g, layer, lam_init,
               batch, q_len, heads, head_dim, v_dim, tq):
    qt = q_len // tq
    qc = q_col_off // head_dim

    def q_spec(c):
        return pl.BlockSpec((tq, head_dim), lambda b, h, i: (q_row_blocks + b * qt + i, qc + 2 * h + c))

    in_specs = [
        pl.BlockSpec((None, 2, head_dim), lambda b, h, i: (layer, 0, 0)),
        pl.BlockSpec((None, 2, head_dim), lambda b, h, i: (layer, 0, 0)),
        pl.BlockSpec((1, v_dim), lambda b, h, i: (0, 0)),
        q_spec(0), q_spec(1),
    ]
    args = [lam_q, lam_k, subln_g.reshape(1, v_dim), q_src, q_src]
    for (k_arr, k_off, v_arr, v_off, seg_len, row_off) in segs:
        kc, vc = k_off // head_dim, v_off // v_dim
        for c in range(2):
            in_specs.append(pl.BlockSpec(
                (seg_len, head_dim), lambda b, h, i, kc=kc, c=c, row_off=row_off: (row_off + b, kc + 2 * h + c)))
            args.append(k_arr)
        in_specs.append(pl.BlockSpec((seg_len, v_dim), lambda b, h, i, vc=vc, row_off=row_off: (row_off + b, vc + h)))
        args.append(v_arr)
    kern = functools.partial(_attn_kernel, n_seg=len(segs), lam_init=lam_init, head_dim=head_dim)
    return pl.pallas_call(
        kern,
        grid=(batch, heads, qt),
        in_specs=in_specs,
        out_specs=pl.BlockSpec((tq, v_dim), lambda b, h, i: (b * qt + i, h)),
        out_shape=jax.ShapeDtypeStruct((batch * q_len, heads * v_dim), BF16),
        compiler_params=_params("arbitrary", "arbitrary", "arbitrary"),
        name="diff_attention",
    )(*args)


def _post_mix_kernel(x_ref, m_ref, gate_ref, gm_ref, g2_ref, sh_ref, sc_ref, rw_ref, rb_ref,
                     xo_ref, hp_ref, lg_ref):
    xn = x_ref[...] + gate_ref[...] * _rms(m_ref[...].astype(F32), gm_ref[...])
    xo_ref[...] = xn
    h2 = _rms(xn, g2_ref[...]) * (1.0 + sc_ref[...]) + sh_ref[...]
    _pack_rows(h2, hp_ref, ROW_TILE)
    nt = (((1,), (1,)), ((), ()))
    lg_ref[...] = lax.dot_general(rw_ref[...], h2.astype(BF16), nt, preferred_element_type=F32) + rb_ref[...]


def _post_mix(x, mixo, mod, g_mix, g_ffn, router_w, router_b, n_rows):
    d = x.shape[1]
    e = router_w.shape[1]
    wr = d // 2 // LANES
    row = pl.BlockSpec((ROW_TILE, d), lambda i: (i, 0))
    vec = pl.BlockSpec((1, d), lambda i: (0, 0))
    return pl.pallas_call(
        _post_mix_kernel,
        grid=(n_rows // ROW_TILE,),
        in_specs=[row, row, mod.spec(2), vec, vec, mod.spec(3), mod.spec(4),
                  pl.BlockSpec((e, d), lambda i: (0, 0)), pl.BlockSpec((e, 1), lambda i: (0, 0))],
        out_specs=[row, pl.BlockSpec((ROW_TILE * wr, LANES), lambda i: (i, 0)),
                   pl.BlockSpec((e, ROW_TILE), lambda i: (0, i))],
        out_shape=[jax.ShapeDtypeStruct((n_rows, d), F32),
                   jax.ShapeDtypeStruct((n_rows * wr, LANES), U32),
                   jax.ShapeDtypeStruct((e, n_rows), F32)],
        compiler_params=_params("arbitrary"),
        name="post_mix",
    )(x, mixo, mod.table, g_mix.reshape(1, d), g_ffn.reshape(1, d), mod.table, mod.table,
      router_w.T.astype(BF16), router_b.reshape(e, 1))


def _router_kernel(lg_ref, idx_ref, pos_ref, gate_ref, cnt_ref, carry_ref):
    i = pl.program_id(0)

    @pl.when(i == 0)
    def _():
        carry_ref[...] = jnp.zeros_like(carry_ref)

    work = lg_ref[...]
    e, tr = work.shape
    e_iota = lax.broadcasted_iota(I32, (e, tr), 0)
    vals, idxs = [], []
    chosen = jnp.zeros((e, tr), dtype=jnp.bool_)
    for _ in range(TOP_K):
        m = jnp.max(work, axis=0, keepdims=True)
        idx = jnp.min(jnp.where(work == m, e_iota, e), axis=0, keepdims=True)
        sel = e_iota == idx
        vals.append(m)
        idxs.append(idx)
        chosen = jnp.logical_or(chosen, sel)
        work = jnp.where(sel, -jnp.inf, work)
    ex = [jnp.exp(v - vals[0]) for v in vals]
    denom = functools.reduce(jnp.add, ex)
    onehot = jnp.where(chosen, 1.0, 0.0)
    before = (lax.broadcasted_iota(I32, (tr, tr), 0) < lax.broadcasted_iota(I32, (tr, tr), 1))
    rank = jnp.dot(onehot.astype(BF16), jnp.where(before, 1.0, 0.0).astype(BF16), preferred_element_type=F32)
    rank = rank + carry_ref[:, 0:1]
    for k in range(TOP_K):
        idx_ref[k:k + 1, :] = idxs[k]
        gate_ref[k:k + 1, :] = ex[k] / denom
        pos_ref[k:k + 1, :] = jnp.sum(jnp.where(e_iota == idxs[k], rank, 0.0), axis=0, keepdims=True).astype(I32)
    carry_ref[...] = carry_ref[...] + jnp.sum(onehot, axis=1, keepdims=True)
    cnt_ref[...] = carry_ref[...]


def _router(logits_t):
    e, n_tok = logits_t.shape
    tr = 512
    out = pl.BlockSpec((TOP_K, tr), lambda i: (0, i))
    return pl.pallas_call(
        _router_kernel,
        grid=(n_tok // tr,),
        in_specs=[pl.BlockSpec((e, tr), lambda i: (0, i))],
        out_specs=[out, out, out, pl.BlockSpec((e, LANES), lambda i: (0, 0))],
        out_shape=[jax.ShapeDtypeStruct((TOP_K, n_tok), I32), jax.ShapeDtypeStruct((TOP_K, n_tok), I32),
                   jax.ShapeDtypeStruct((TOP_K, n_tok), F32), jax.ShapeDtypeStruct((e, LANES), F32)],
        scratch_shapes=[pltpu.VMEM((e, LANES), F32)],
        compiler_params=_params("arbitrary"),
        name="router",
    )(logits_t)


def _dispatch_kernel(start_ref, cnt_ref, pad_ref, idx_ref, pos_ref, hp_ref, xs_ref, zero_ref, sem, *,
                     tile, wr, n_exp):
    i = pl.program_id(0)

    def row_copy(src, dst_row):
        return pltpu.make_async_copy(src, xs_ref.at[pl.ds(pl.multiple_of(dst_row * wr, wr), wr)], sem)

    @pl.when(i == 0)
    def _():
        zero_ref[...] = jnp.zeros_like(zero_ref)
        for phase in ("start", "wait"):
            def pad_rows(ex, c, phase=phase):
                def pad_row(r, c2):
                    cp = row_copy(zero_ref, start_ref[ex] + r)
                    cp.start() if phase == "start" else cp.wait()
                    return c2
                return lax.fori_loop(cnt_ref[ex], pad_ref[ex], pad_row, c)
            lax.fori_loop(0, n_exp, pad_rows, 0)

    def token(t, c, phase):
        src = hp_ref.at[pl.ds(pl.multiple_of(t * wr, wr), wr)]
        for k in range(TOP_K):
            a = t * TOP_K + k
            cp = row_copy(src, start_ref[idx_ref[a]] + pos_ref[a])
            cp.start() if phase == "start" else cp.wait()
        return c

    lax.fori_loop(0, tile, functools.partial(token, phase="start"), 0)
    lax.fori_loop(0, tile, functools.partial(token, phase="wait"), 0)


def _dispatch(hp, idx_flat, pos_flat, pad_start, counts, padded, n_tok, n_rows, wr):
    tile = 512
    n_exp = pad_start.shape[0]
    kern = functools.partial(_dispatch_kernel, tile=tile, wr=wr, n_exp=n_exp)
    flat = pl.BlockSpec((tile * TOP_K,), lambda i, *_: (i,), memory_space=pltpu.SMEM)
    return pl.pallas_call(
        kern,
        grid_spec=pltpu.PrefetchScalarGridSpec(
            num_scalar_prefetch=3,
            grid=(n_tok // tile,),
            in_specs=[flat, flat, pl.BlockSpec((tile * wr, LANES), lambda i, *_: (i, 0))],
            out_specs=pl.BlockSpec(memory_space=pl.ANY),
            scratch_shapes=[pltpu.VMEM((wr, LANES), U32), pltpu.SemaphoreType.DMA(())],
        ),
        out_shape=jax.ShapeDtypeStruct((n_rows * wr, LANES), U32),
        compiler_params=_params("arbitrary"),
        name="moe_dispatch",
    )(pad_start, counts, padded, idx_flat, pos_flat, hp)


def _split_kernel(w_ref, o_ref):
    half = PAIR_GROUP // 2
    src = lax.broadcasted_iota(I32, (PAIR_GROUP, PAIR_GROUP), 0)
    dst = lax.broadcasted_iota(I32, (PAIR_GROUP, PAIR_GROUP), 1)
    perm = jnp.where(src == jnp.where(dst < half, 2 * dst, 2 * (dst - half) + 1), 1.0, 0.0).astype(BF16)
    for g in range(w_ref.shape[1] // PAIR_GROUP):
        cols = slice(g * PAIR_GROUP, (g + 1) * PAIR_GROUP)
        o_ref[:, cols] = jnp.dot(w_ref[:, cols].astype(BF16), perm, preferred_element_type=F32).astype(BF16)


def _split_gate_up(w):
    n, d, cols = w.shape
    tk = _tile(d, 1024, 512, 256)
    blk = pl.BlockSpec((None, tk, cols), lambda e, i: (e, i, 0))
    return pl.pallas_call(
        _split_kernel,
        grid=(n, d // tk),
        in_specs=[blk],
        out_specs=blk,
        out_shape=jax.ShapeDtypeStruct((n, d, cols), BF16),
        compiler_params=_params("arbitrary", "arbitrary"),
        name="split_gate_up",
    )(w)


def _expert_kernel(blk_e_ref, used_ref, xs_ref, wgu_ref, bgu_ref, wd_ref, bd_ref, y_ref, *, wr):
    j = pl.program_id(0)

    @pl.when(j < used_ref[0])
    def _():
        lo, hi = _unpack_rows(xs_ref, MOE_BLOCK, wr)
        xb = jnp.concatenate([lo.astype(BF16), hi.astype(BF16)], axis=1)
        gu = jnp.dot(xb, wgu_ref[...], preferred_element_type=F32) + bgu_ref[...]
        half = PAIR_GROUP // 2
        groups = gu.shape[1] // PAIR_GROUP
        gate = jnp.concatenate([gu[:, g * PAIR_GROUP:g * PAIR_GROUP + half] for g in range(groups)], axis=1)
        up = jnp.concatenate([gu[:, g * PAIR_GROUP + half:(g + 1) * PAIR_GROUP] for g in range(groups)], axis=1)
        gate = jnp.minimum(gate, SWIGLU_LIMIT)
        up = jnp.clip(up, -SWIGLU_LIMIT, SWIGLU_LIMIT)
        act = (up + 1.0) * (gate * jax.nn.sigmoid(SWIGLU_ALPHA * gate))
        out = jnp.dot(act.astype(BF16), wd_ref[...], preferred_element_type=F32) + bd_ref[...]
        _pack_rows(out, y_ref, MOE_BLOCK)


def _experts(xs, blk_e, n_used, w_gu, b_gu, w_down, b_down, layer, n_exp, n_blk, wr):
    _, d, cols = w_gu.shape
    de = cols // 2
    base = layer * n_exp
    rows = pl.BlockSpec((MOE_BLOCK * wr, LANES), lambda j, be, nu: (jnp.minimum(j, nu[0] - 1), 0))

    def per_expert(shape):
        return pl.BlockSpec((None,) + shape, lambda j, be, nu: (base + be[j], 0, 0))

    kern = functools.partial(_expert_kernel, wr=wr)
    return pl.pallas_call(
        kern,
        grid_spec=pltpu.PrefetchScalarGridSpec(
            num_scalar_prefetch=2,
            grid=(n_blk,),
            in_specs=[rows, per_expert((d, cols)), per_expert((1, cols)), per_expert((de, d)), per_expert((1, d))],
            out_specs=rows,
        ),
        out_shape=jax.ShapeDtypeStruct((n_blk * MOE_BLOCK * wr, LANES), U32),
        compiler_params=_params("arbitrary"),
        name="moe_experts",
    )(blk_e, n_used, xs, w_gu, b_gu, w_down, b_down)


def _combine_kernel(start_ref, idx_ref, pos_ref, x_ref, gates_ref, g2_ref, gn_ref, y_ref, xo_ref, buf_ref, sem, *,
                    tile, wr):
    def token(t, c, phase):
        for k in range(TOP_K):
            a = t * TOP_K + k
            row = start_ref[idx_ref[a]] + pos_ref[a]
            cp = pltpu.make_async_copy(y_ref.at[pl.ds(pl.multiple_of(row * wr, wr), wr)],
                                       buf_ref.at[k, pl.ds(pl.multiple_of(t * wr, wr), wr)], sem)
            cp.start() if phase == "start" else cp.wait()
        return c

    lax.fori_loop(0, tile, functools.partial(token, phase="start"), 0)
    lax.fori_loop(0, tile, functools.partial(token, phase="wait"), 0)
    f_lo = f_hi = None
    for k in range(TOP_K):
        lo, hi = _unpack_rows(buf_ref, tile, wr, lead=k)
        gk = gates_ref[:, k:k + 1]
        f_lo = lo * gk if f_lo is None else f_lo + lo * gk
        f_hi = hi * gk if f_hi is None else f_hi + hi * gk
    f = jnp.concatenate([f_lo, f_hi], axis=1)
    xo_ref[...] = x_ref[...] + g2_ref[...] * _rms(f, gn_ref[...])


def _combine(x, y, idx_flat, pos_flat, pad_start, gates, mod, g_out, n_rows, wr):
    d = x.shape[1]
    tile = ROW_TILE
    row = pl.BlockSpec((tile, d), lambda i, *_: (i, 0))
    flat = pl.BlockSpec((tile * TOP_K,), lambda i, *_: (i,), memory_space=pltpu.SMEM)
    g2 = mod.spec(5)
    kern = functools.partial(_combine_kernel, tile=tile, wr=wr)
    return pl.pallas_call(
        kern,
        grid_spec=pltpu.PrefetchScalarGridSpec(
            num_scalar_prefetch=1,
            grid=(n_rows // tile,),
            in_specs=[flat, flat, row, pl.BlockSpec((tile, TOP_K), lambda i, *_: (i, 0)),
                      pl.BlockSpec(g2.block_shape, lambda i, *_: g2.index_map(i)),
                      pl.BlockSpec((1, d), lambda i, *_: (0, 0)), pl.BlockSpec(memory_space=pl.ANY)],
            out_specs=row,
            scratch_shapes=[pltpu.VMEM((TOP_K, tile * wr, LANES), U32), pltpu.SemaphoreType.DMA(())],
        ),
        out_shape=jax.ShapeDtypeStruct((n_rows, d), F32),
        compiler_params=_params("arbitrary"),
        name="moe_combine",
    )(pad_start, idx_flat, pos_flat, x, gates, mod.table, g_out.reshape(1, d), y)


def _moe(x_new, hp, logits_t, mod, g_out, w_gu, b_gu, w_down, b_down, layer, n_exp, n_rows, wr):
    idx, pos, gates, counts = _router(logits_t)
    counts = counts[:, 0].astype(I32)
    padded = (counts + MOE_BLOCK - 1) // MOE_BLOCK * MOE_BLOCK
    pad_end = jnp.cumsum(padded)
    pad_start = pad_end - padded
    n_blk = -(-(n_rows * TOP_K) // MOE_BLOCK) + n_exp
    blk_first = jnp.arange(n_blk, dtype=I32) * MOE_BLOCK
    blk_e = jnp.minimum(jnp.sum(blk_first[:, None] >= pad_end[None, :], axis=1), n_exp - 1).astype(I32)
    n_used = (pad_end[-1:] // MOE_BLOCK).astype(I32)
    idx_flat, pos_flat = idx.T.reshape(-1), pos.T.reshape(-1)
    xs = _dispatch(hp, idx_flat, pos_flat, pad_start, counts, padded, n_rows, n_blk * MOE_BLOCK, wr)
    y = _experts(xs, blk_e, n_used, w_gu, b_gu, w_down, b_down, layer, n_exp, n_blk, wr)
    return _combine(x_new, y, idx_flat, pos_flat, pad_start, gates.T, mod, g_out, n_rows, wr)


def kernel(x, c, ctx, c_ctx, w_mod, b_mod, norm_g, w_in, gmlp_ln_g, gmlp_ln_b, w_spatial, b_spatial,
           lambda_q, lambda_k, subln_g, w_out, router_w, router_b, w_gate_up, b_gate_up, w_down, b_down):
    batch, seq, d = x.shape
    n_ctx = ctx.shape[1]
    depth = w_in.shape[0]
    in_cols = w_in.shape[2]
    a_width = gmlp_ln_g.shape[1]
    head_dim = lambda_q.shape[2]
    v_dim = subln_g.shape[1]
    heads = (in_cols - 2 * a_width) // (4 * head_dim + v_dim)
    qk_width = heads * 2 * head_dim
    q_off = 2 * a_width
    k_off = q_off + qk_width
    val_off = k_off + qk_width
    n_exp = router_w.shape[2]
    de = w_down.shape[2]
    wr = d // 2 // LANES
    n_lat, n_cx = batch * seq, batch * n_ctx
    n_all = n_lat + n_cx
    assert batch < MOD_ROWS and seq % ROW_TILE == 0 and n_ctx % ROW_TILE == 0
    assert seq % n_ctx == 0 and d % (2 * LANES) == 0

    xa = jnp.concatenate([x.reshape(n_lat, d), ctx.reshape(n_cx, d)], axis=0)
    act = jnp.zeros((MOD_ROWS, d), F32).at[:batch].set(c).at[batch].set(c_ctx)
    mod_table = _modulation(act, w_mod, b_mod).reshape(depth * MOD_ROWS * 6, 1, d)
    tables = _rope_tables(seq, head_dim)
    tm = _tile(n_all, 1024, ROW_TILE)
    assert (2 * de) % PAIR_GROUP == 0

    w_in_b, w_out_b = w_in.astype(BF16), w_out.astype(BF16)
    w_gu = _split_gate_up(w_gate_up.reshape(depth * n_exp, d, 2 * de))
    groups = 2 * de // PAIR_GROUP
    b_gu = b_gate_up.reshape(depth * n_exp, groups, PAIR_GROUP // 2, 2)
    b_gu = b_gu.transpose(0, 1, 3, 2).reshape(depth * n_exp, 1, 2 * de)
    w_dn = w_down.astype(BF16).reshape(depth * n_exp, de, d)
    b_dn = b_down.reshape(depth * n_exp, 1, d)

    for l in range(depth):
        last = l == depth - 1
        lam_init = 0.8 - 0.6 * math.exp(-0.3 * l)
        mod = _Mod(mod_table, l, n_lat, seq, batch, ROW_TILE)
        n_act = n_lat if last else n_all

        h = _norm_mod(xa, norm_g[l, 0], mod, n_all)
        z = _matmul([h], w_in_b, l, n_all, tm, _tile(in_cols, 1024, 512, 256), "in_proj")

        a_mix = _gmlp(z, gmlp_ln_g[l], gmlp_ln_b[l], w_spatial[l], b_spatial[l], n_act)
        qk = _rope(z, tables, n_lat, seq, q_off, 2 * qk_width, head_dim)
        b_mix = _attention_latent(qk, z, k_off, val_off, lambda_q, lambda_k, subln_g[l], l, lam_init,
                                  batch, seq, n_ctx, heads, head_dim, v_dim, min(512, seq), min(512, seq))
        if not last:
            tqc = min(256, n_ctx)
            ctx_seg = (z, k_off, z, val_off, n_ctx, n_lat // n_ctx)
            b_ctx = _attention(z, q_off, n_lat // tqc, [ctx_seg], lambda_q, lambda_k, subln_g[l], l, lam_init,
                               batch, n_ctx, heads, head_dim, v_dim, tqc)
            b_mix = jnp.concatenate([b_mix, b_ctx], axis=0)
        mixo = _matmul([a_mix, b_mix], w_out_b, l, n_act,
                       _tile(n_act, 1024, ROW_TILE), _tile(d, 512, 256), "out_proj")

        x_new, hp, logits_t = _post_mix(xa, mixo, mod, norm_g[l, 1], norm_g[l, 2], router_w[l], router_b[l], n_act)
        xa = _moe(x_new, hp, logits_t, mod, norm_g[l, 3], w_gu, b_gu, w_dn, b_dn, l, n_exp, n_act, wr)
    return xa[:n_lat].reshape(batch, seq, d)
```

```python
import functools
import math

import jax
import jax.numpy as jnp
from jax import lax
from jax.experimental import pallas as pl
from jax.experimental.pallas import tpu as pltpu

F32 = jnp.float32
BF16 = jnp.bfloat16
U32 = jnp.uint32
I32 = jnp.int32

LANES = 128
MOD_ROWS = 8
TOP_K = 4
GRID_W = 64
ROPE_THETA = 10000.0
EPS = 1e-6
SWIGLU_LIMIT = 7.0
SWIGLU_ALPHA = 1.702
VMEM_LIMIT = 56 * 1024 * 1024

ROW_TILE = 256
MOE_BLOCK = 256
PAIR_GROUP = 256
GATHER_PITCH_PAD = 4


def _tile(n, *prefs):
    return next(t for t in prefs if n % t == 0)


def _params(*sem):
    return pltpu.CompilerParams(dimension_semantics=sem, vmem_limit_bytes=VMEM_LIMIT)


def _rms(v, g):
    return v * lax.rsqrt(jnp.mean(v * v, axis=-1, keepdims=True) + EPS) * g


def _pack_rows(v, out_ref, n_rows):
    d = v.shape[1]
    half = d // 2
    words_rows = half // LANES
    bits = pltpu.bitcast(v.astype(BF16).astype(F32), U32)
    w = (bits[:, half:] & jnp.uint32(0xFFFF0000)) | (bits[:, :half] >> 16)
    for s in range(words_rows):
        out_ref[pl.ds(s, n_rows, stride=words_rows), :] = w[:, s * LANES:(s + 1) * LANES]


def _unpack_rows(ref, n_rows, words_rows, lead=None, pitch=None):
    lo, hi = [], []
    for s in range(words_rows):
        idx = (pl.ds(s, n_rows, stride=pitch or words_rows), slice(None))
        w = ref[idx] if lead is None else ref[(lead,) + idx]
        lo.append(pltpu.bitcast(w << 16, F32))
        hi.append(pltpu.bitcast(w & jnp.uint32(0xFFFF0000), F32))
    return jnp.concatenate(lo, axis=1), jnp.concatenate(hi, axis=1)


def _mod_kernel(a_ref, w_ref, b_ref, o_ref):
    a = a_ref[...]
    a = (a * jax.nn.sigmoid(a)).astype(BF16)
    o_ref[...] = jnp.dot(a, w_ref[...].astype(BF16), preferred_element_type=F32) + b_ref[...]


def _modulation(act, w_mod, b_mod):
    depth, d, n = w_mod.shape
    tn = 512
    return pl.pallas_call(
        _mod_kernel,
        grid=(depth, n // tn),
        in_specs=[
            pl.BlockSpec((MOD_ROWS, d), lambda l, j: (0, 0)),
            pl.BlockSpec((None, d, tn), lambda l, j: (l, 0, j)),
            pl.BlockSpec((None, 1, tn), lambda l, j: (l, 0, j)),
        ],
        out_specs=pl.BlockSpec((None, MOD_ROWS, tn), lambda l, j: (l, 0, j)),
        out_shape=jax.ShapeDtypeStruct((depth, MOD_ROWS, n), F32),
        compiler_params=_params("arbitrary", "arbitrary"),
        name="modulation",
    )(act, w_mod, b_mod.reshape(depth, 1, n))


class _Mod:
    def __init__(self, table, layer, n_lat_rows, seq, batch, tile):
        self.table = table
        self.layer = layer
        self.lat_tiles = n_lat_rows // tile
        self.seq_tiles = seq // tile
        self.batch = batch
        self.d = table.shape[-1]

    def spec(self, which):
        base = self.layer * MOD_ROWS * 6 + which

        def index(i):
            row = jnp.where(i < self.lat_tiles, i // self.seq_tiles, self.batch)
            return (base + row * 6, 0, 0)

        return pl.BlockSpec((None, 1, self.d), index)


def _norm_mod_kernel(x_ref, g_ref, sh_ref, sc_ref, o_ref):
    y = _rms(x_ref[...], g_ref[...])
    o_ref[...] = (y * (1.0 + sc_ref[...]) + sh_ref[...]).astype(o_ref.dtype)


def _norm_mod(x, g, mod, n_rows):
    d = x.shape[1]
    row = pl.BlockSpec((ROW_TILE, d), lambda i: (i, 0))
    return pl.pallas_call(
        _norm_mod_kernel,
        grid=(n_rows // ROW_TILE,),
        in_specs=[row, pl.BlockSpec((1, d), lambda i: (0, 0)), mod.spec(0), mod.spec(1)],
        out_specs=row,
        out_shape=jax.ShapeDtypeStruct((n_rows, d), BF16),
        compiler_params=_params("arbitrary"),
        name="norm_mod",
    )(x, g.reshape(1, d), mod.table, mod.table)


def _matmul_kernel(*refs):
    n_pairs = (len(refs) - 1) // 2
    o_ref = refs[-1]
    acc = None
    for p in range(n_pairs):
        part = jnp.dot(refs[p][...], refs[n_pairs + p][...], preferred_element_type=F32)
        acc = part if acc is None else acc + part
    o_ref[...] = acc.astype(o_ref.dtype)


def _matmul(a_list, w, layer, n_rows, tm, tn, name):
    _, k_total, n = w.shape
    kb = k_total // len(a_list)
    assert all(a.shape[1] == kb for a in a_list)
    in_specs = [pl.BlockSpec((tm, kb), lambda i, j: (i, 0)) for _ in a_list]
    in_specs += [pl.BlockSpec((None, kb, tn), lambda i, j, p=p: (layer, p, j)) for p in range(len(a_list))]
    return pl.pallas_call(
        _matmul_kernel,
        grid=(n_rows // tm, n // tn),
        in_specs=in_specs,
        out_specs=pl.BlockSpec((tm, tn), lambda i, j: (i, j)),
        out_shape=jax.ShapeDtypeStruct((n_rows, n), BF16),
        compiler_params=_params("arbitrary", "arbitrary"),
        name=name,
    )(*a_list, *([w] * len(a_list)))


def _gmlp_kernel(z_ref, lng_ref, lnb_ref, ws_ref, bs_ref, o_ref, *, heads, head_dim, chunk, chunks):
    width = heads * head_dim

    def one_chunk(ci, carry):
        rows = pl.ds(pl.multiple_of(ci * chunk, chunk), chunk)
        for h in range(heads):
            cu = slice(h * head_dim, (h + 1) * head_dim)
            cv = slice(width + h * head_dim, width + (h + 1) * head_dim)
            u = jax.nn.gelu(z_ref[rows, cu].astype(F32))
            v = jax.nn.gelu(z_ref[rows, cv].astype(F32))
            vc = v - jnp.mean(v, axis=-1, keepdims=True)
            v = vc * lax.rsqrt(jnp.mean(vc * vc, axis=-1, keepdims=True) + EPS)
            v = v * lng_ref[:, cu] + lnb_ref[:, cu]
            mixed = jnp.dot(ws_ref[h], v.astype(BF16), preferred_element_type=F32) + bs_ref[:, h:h + 1]
            o_ref[rows, cu] = (u * mixed).astype(o_ref.dtype)
        return carry

    lax.fori_loop(0, chunks, one_chunk, 0)


def _gmlp(z, ln_g, ln_b, w_s, b_s, n_rows):
    heads, chunk, _ = w_s.shape
    width = ln_g.shape[0]
    chunks = 2
    tile = chunk * chunks
    kern = functools.partial(_gmlp_kernel, heads=heads, head_dim=width // heads, chunk=chunk, chunks=chunks)
    return pl.pallas_call(
        kern,
        grid=(n_rows // tile,),
        in_specs=[
            pl.BlockSpec((tile, 2 * width), lambda i: (i, 0)),
            pl.BlockSpec((1, width), lambda i: (0, 0)),
            pl.BlockSpec((1, width), lambda i: (0, 0)),
            pl.BlockSpec((heads, chunk, chunk), lambda i: (0, 0, 0)),
            pl.BlockSpec((chunk, heads), lambda i: (0, 0)),
        ],
        out_specs=pl.BlockSpec((tile, width), lambda i: (i, 0)),
        out_shape=jax.ShapeDtypeStruct((n_rows, width), BF16),
        compiler_params=_params("arbitrary"),
        name="gmlp",
    )(z, ln_g.reshape(1, width), ln_b.reshape(1, width), w_s.astype(BF16), b_s.T)


def _rope_tables(seq, head_dim):
    rows = seq // GRID_W
    row = jnp.repeat(jnp.arange(rows, dtype=F32), GRID_W)
    col = jnp.tile(jnp.arange(GRID_W, dtype=F32), rows)
    axis_dim = head_dim // 2
    inv_freq = jnp.power(ROPE_THETA, -jnp.arange(0, axis_dim, 2, dtype=F32) / axis_dim)
    ang_r = row[:, None] * inv_freq
    ang_c = col[:, None] * inv_freq
    ang = jnp.concatenate([ang_r, ang_r, ang_c, ang_c], axis=-1)
    cos, sin = jnp.cos(ang), jnp.sin(ang)
    quarter = (jnp.arange(head_dim) // (head_dim // 4)) % 2
    sin_up = jnp.where(quarter == 0, -sin, 0.0)
    sin_dn = jnp.where(quarter == 1, sin, 0.0)
    return cos, sin_up, sin_dn


def _rope_kernel(z_ref, cos_ref, up_ref, dn_ref, o_ref, *, head_dim, groups):
    cos, up, dn = cos_ref[...], up_ref[...], dn_ref[...]
    q4 = head_dim // 4
    q_scale = head_dim ** -0.5 * math.log2(math.e)
    for gidx in range(groups):
        cols = slice(gidx * head_dim, (gidx + 1) * head_dim)
        t = z_ref[:, cols].astype(F32)
        r = t * cos + pltpu.roll(t, head_dim - q4, 1) * up + pltpu.roll(t, q4, 1) * dn
        if gidx < groups // 2:
            r = r * q_scale
        o_ref[:, cols] = r.astype(o_ref.dtype)


def _rope(z, tables, n_rows, seq, col_off, width, head_dim):
    tr = min(512, seq)
    seq_tiles = seq // tr
    tab = pl.BlockSpec((tr, head_dim), lambda i: (i % seq_tiles, 0))
    kern = functools.partial(_rope_kernel, head_dim=head_dim, groups=width // head_dim)
    return pl.pallas_call(
        kern,
        grid=(n_rows // tr,),
        in_specs=[pl.BlockSpec((tr, width), lambda i: (i, col_off // width)), tab, tab, tab],
        out_specs=pl.BlockSpec((tr, width), lambda i: (i, 0)),
        out_shape=jax.ShapeDtypeStruct((n_rows, width), BF16),
        compiler_params=_params("arbitrary"),
        name="rope",
    )(z, *tables)


def _attn_kernel(*refs, n_seg, lam_init, head_dim):
    lq_ref, lk_ref, g_ref, q0_ref, q1_ref = refs[:5]
    seg = refs[5:5 + 3 * n_seg]
    o_ref = refs[-1]
    lam = (jnp.exp(jnp.sum(lq_ref[0:1, :] * lk_ref[0:1, :], axis=-1, keepdims=True))
           - jnp.exp(jnp.sum(lq_ref[1:2, :] * lk_ref[1:2, :], axis=-1, keepdims=True)) + lam_init)
    scale = head_dim ** -0.5
    nt = (((1,), (1,)), ((), ()))
    a = [None] * n_seg
    for c, q_ref in enumerate((q0_ref, q1_ref)):
        q = q_ref[...]
        s = [lax.dot_general(q, seg[3 * i + c][...], nt, preferred_element_type=F32) * scale for i in range(n_seg)]
        m = functools.reduce(jnp.maximum, [jnp.max(si, axis=-1, keepdims=True) for si in s])
        p = [jnp.exp(si - m) for si in s]
        denom = functools.reduce(jnp.add, [jnp.sum(pi, axis=-1, keepdims=True) for pi in p])
        if c == 0:
            a = [pi / denom for pi in p]
        else:
            a = [ai - lam * (pi / denom) for ai, pi in zip(a, p)]
    o = functools.reduce(jnp.add, [jnp.dot(a[i].astype(BF16), seg[3 * i + 2][...], preferred_element_type=F32)
                                   for i in range(n_seg)])
    o_ref[...] = (_rms(o, g_ref[...]) * (1.0 - lam_init)).astype(o_ref.dtype)


def _flash_kernel(lq_ref, lk_ref, g_ref, q0_ref, q1_ref, kc0_ref, kc1_ref, vc_ref, kl0_ref, kl1_ref, vl_ref,
                  o_ref, *, lam_init, tk):
    lam = (jnp.exp(jnp.sum(lq_ref[0:1, :] * lk_ref[0:1, :], axis=-1, keepdims=True))
           - jnp.exp(jnp.sum(lq_ref[1:2, :] * lk_ref[1:2, :], axis=-1, keepdims=True)) + lam_init)
    nt = (((1,), (1,)), ((), ()))
    chunks = kl0_ref.shape[0] // tk
    qs = (q0_ref[...], q1_ref[...])
    kls = (kl0_ref, kl1_ref)

    def first(q, kc_ref):
        s = lax.dot_general(q, kc_ref[...], nt, preferred_element_type=F32)
        m = jnp.max(s, axis=-1, keepdims=True)
        p = jnp.exp2(s - m)
        return m, jnp.sum(p, axis=-1, keepdims=True), jnp.dot(p.astype(BF16), vc_ref[...], preferred_element_type=F32)

    def chunk(j, carry):
        rows = pl.ds(pl.multiple_of(j * tk, tk), tk)
        v = vl_ref[rows, :]
        new = []
        for c in range(2):
            m, l, acc = carry[c]
            s = lax.dot_general(qs[c], kls[c][rows, :], nt, preferred_element_type=F32)
            m_new = jnp.maximum(m, jnp.max(s, axis=-1, keepdims=True))
            alpha = jnp.exp2(m - m_new)
            p = jnp.exp2(s - m_new)
            l = alpha * l + jnp.sum(p, axis=-1, keepdims=True)
            acc = alpha * acc + jnp.dot(p.astype(BF16), v, preferred_element_type=F32)
            new.append((m_new, l, acc))
        return tuple(new)

    carry = lax.fori_loop(0, chunks, chunk, (first(qs[0], kc0_ref), first(qs[1], kc1_ref)), unroll=True)
    outs = [acc * (1.0 / l) for (_, l, acc) in carry]
    o = outs[0] - lam * outs[1]
    o_ref[...] = (_rms(o, g_ref[...]) * (1.0 - lam_init)).astype(o_ref.dtype)


def _attention_latent(qk, z, k_off, val_off, lam_q, lam_k, subln_g, layer, lam_init,
                      batch, seq, n_ctx, heads, head_dim, v_dim, tq, tk):
    qt = seq // tq
    kq = heads * 2
    kc, vc = k_off // head_dim, val_off // v_dim
    ctx_blk = batch * seq // n_ctx
    lam_spec = pl.BlockSpec((None, 2, head_dim), lambda b, h, i: (layer, 0, 0))
    in_specs = [
        lam_spec, lam_spec, pl.BlockSpec((1, v_dim), lambda b, h, i: (0, 0)),
        pl.BlockSpec((tq, head_dim), lambda b, h, i: (b * qt + i, 2 * h)),
        pl.BlockSpec((tq, head_dim), lambda b, h, i: (b * qt + i, 2 * h + 1)),
        pl.BlockSpec((n_ctx, head_dim), lambda b, h, i: (ctx_blk + b, kc + 2 * h)),
        pl.BlockSpec((n_ctx, head_dim), lambda b, h, i: (ctx_blk + b, kc + 2 * h + 1)),
        pl.BlockSpec((n_ctx, v_dim), lambda b, h, i: (ctx_blk + b, vc + h)),
        pl.BlockSpec((seq, head_dim), lambda b, h, i: (b, kq + 2 * h)),
        pl.BlockSpec((seq, head_dim), lambda b, h, i: (b, kq + 2 * h + 1)),
        pl.BlockSpec((seq, v_dim), lambda b, h, i: (b, vc + h)),
    ]
    kern = functools.partial(_flash_kernel, lam_init=lam_init, tk=tk)
    return pl.pallas_call(
        kern,
        grid=(batch, heads, qt),
        in_specs=in_specs,
        out_specs=pl.BlockSpec((tq, v_dim), lambda b, h, i: (b * qt + i, h)),
        out_shape=jax.ShapeDtypeStruct((batch * seq, heads * v_dim), BF16),
        compiler_params=_params("arbitrary", "arbitrary", "arbitrary"),
        name="flash_diff_attention",
    )(lam_q, lam_k, subln_g.reshape(1, v_dim), qk, qk, z, z, z, qk, qk, z)


def _attention(q_src, q_col_off, q_row_blocks, segs, lam_q, lam_k, subln_g, layer, lam_init,
               batch, q_len, heads, head_dim, v_dim, tq):
    qt = q_len // tq
    qc = q_col_off // head_dim

    def q_spec(c):
        return pl.BlockSpec((tq, head_dim), lambda b, h, i: (q_row_blocks + b * qt + i, qc + 2 * h + c))

    in_specs = [
        pl.BlockSpec((None, 2, head_dim), lambda b, h, i: (layer, 0, 0)),
        pl.BlockSpec((None, 2, head_dim), lambda b, h, i: (layer, 0, 0)),
        pl.BlockSpec((1, v_dim), lambda b, h, i: (0, 0)),
        q_spec(0), q_spec(1),
    ]
    args = [lam_q, lam_k, subln_g.reshape(1, v_dim), q_src, q_src]
    for (k_arr, k_off, v_arr, v_off, seg_len, row_off) in segs:
        kc, vc = k_off // head_dim, v_off // v_dim
        for c in range(2):
            in_specs.append(pl.BlockSpec(
                (seg_len, head_dim), lambda b, h, i, kc=kc, c=c, row_off=row_off: (row_off + b, kc + 2 * h + c)))
            args.append(k_arr)
        in_specs.append(pl.BlockSpec((seg_len, v_dim), lambda b, h, i, vc=vc, row_off=row_off: (row_off + b, vc + h)))
        args.append(v_arr)
    kern = functools.partial(_attn_kernel, n_seg=len(segs), lam_init=lam_init, head_dim=head_dim)
    return pl.pallas_call(
        kern,
        grid=(batch, heads, qt),
        in_specs=in_specs,
        out_specs=pl.BlockSpec((tq, v_dim), lambda b, h, i: (b * qt + i, h)),
        out_shape=jax.ShapeDtypeStruct((batch * q_len, heads * v_dim), BF16),
        compiler_params=_params("arbitrary", "arbitrary", "arbitrary"),
        name="diff_attention",
    )(*args)


def _post_mix_kernel(x_ref, m_ref, gate_ref, gm_ref, g2_ref, sh_ref, sc_ref, rw_ref, rb_ref,
                     xo_ref, hp_ref, lg_ref):
    xn = x_ref[...] + gate_ref[...] * _rms(m_ref[...].astype(F32), gm_ref[...])
    xo_ref[...] = xn
    h2 = _rms(xn, g2_ref[...]) * (1.0 + sc_ref[...]) + sh_ref[...]
    _pack_rows(h2, hp_ref, ROW_TILE)
    nt = (((1,), (1,)), ((), ()))
    lg_ref[...] = lax.dot_general(rw_ref[...], h2.astype(BF16), nt, preferred_element_type=F32) + rb_ref[...]


def _post_mix(x, mixo, mod, g_mix, g_ffn, router_w, router_b, n_rows):
    d = x.shape[1]
    e = router_w.shape[1]
    wr = d // 2 // LANES
    row = pl.BlockSpec((ROW_TILE, d), lambda i: (i, 0))
    vec = pl.BlockSpec((1, d), lambda i: (0, 0))
    return pl.pallas_call(
        _post_mix_kernel,
        grid=(n_rows // ROW_TILE,),
        in_specs=[row, row, mod.spec(2), vec, vec, mod.spec(3), mod.spec(4),
                  pl.BlockSpec((e, d), lambda i: (0, 0)), pl.BlockSpec((e, 1), lambda i: (0, 0))],
        out_specs=[row, pl.BlockSpec((ROW_TILE * wr, LANES), lambda i: (i, 0)),
                   pl.BlockSpec((e, ROW_TILE), lambda i: (0, i))],
        out_shape=[jax.ShapeDtypeStruct((n_rows, d), F32),
                   jax.ShapeDtypeStruct((n_rows * wr, LANES), U32),
                   jax.ShapeDtypeStruct((e, n_rows), F32)],
        compiler_params=_params("arbitrary"),
        name="post_mix",
    )(x, mixo, mod.table, g_mix.reshape(1, d), g_ffn.reshape(1, d), mod.table, mod.table,
      router_w.T.astype(BF16), router_b.reshape(e, 1))


def _router_kernel(lg_ref, idx_ref, pos_ref, gate_ref, cnt_ref, carry_ref):
    i = pl.program_id(0)

    @pl.when(i == 0)
    def _():
        carry_ref[...] = jnp.zeros_like(carry_ref)

    work = lg_ref[...]
    e, tr = work.shape
    e_iota = lax.broadcasted_iota(I32, (e, tr), 0)
    vals, idxs = [], []
    chosen = jnp.zeros((e, tr), dtype=jnp.bool_)
    for _ in range(TOP_K):
        m = jnp.max(work, axis=0, keepdims=True)
        idx = jnp.min(jnp.where(work == m, e_iota, e), axis=0, keepdims=True)
        sel = e_iota == idx
        vals.append(m)
        idxs.append(idx)
        chosen = jnp.logical_or(chosen, sel)
        work = jnp.where(sel, -jnp.inf, work)
    ex = [jnp.exp(v - vals[0]) for v in vals]
    denom = functools.reduce(jnp.add, ex)
    onehot = jnp.where(chosen, 1.0, 0.0)
    before = (lax.broadcasted_iota(I32, (tr, tr), 0) < lax.broadcasted_iota(I32, (tr, tr), 1))
    rank = jnp.dot(onehot.astype(BF16), jnp.where(before, 1.0, 0.0).astype(BF16), preferred_element_type=F32)
    rank = rank + carry_ref[:, 0:1]
    for k in range(TOP_K):
        idx_ref[k:k + 1, :] = idxs[k]
        gate_ref[k:k + 1, :] = ex[k] / denom
        pos_ref[k:k + 1, :] = jnp.sum(jnp.where(e_iota == idxs[k], rank, 0.0), axis=0, keepdims=True).astype(I32)
    carry_ref[...] = carry_ref[...] + jnp.sum(onehot, axis=1, keepdims=True)
    cnt_ref[...] = carry_ref[...]


def _router(logits_t):
    e, n_tok = logits_t.shape
    tr = 512
    out = pl.BlockSpec((TOP_K, tr), lambda i: (0, i))
    return pl.pallas_call(
        _router_kernel,
        grid=(n_tok // tr,),
        in_specs=[pl.BlockSpec((e, tr), lambda i: (0, i))],
        out_specs=[out, out, out, pl.BlockSpec((e, LANES), lambda i: (0, 0))],
        out_shape=[jax.ShapeDtypeStruct((TOP_K, n_tok), I32), jax.ShapeDtypeStruct((TOP_K, n_tok), I32),
                   jax.ShapeDtypeStruct((TOP_K, n_tok), F32), jax.ShapeDtypeStruct((e, LANES), F32)],
        scratch_shapes=[pltpu.VMEM((e, LANES), F32)],
        compiler_params=_params("arbitrary"),
        name="router",
    )(logits_t)


def _dispatch_kernel(start_ref, cnt_ref, pad_ref, idx_ref, pos_ref, hp_ref, xs_ref, zero_ref, sem, *,
                     tile, wr, n_exp):
    i = pl.program_id(0)

    def row_copy(src, dst_row):
        return pltpu.make_async_copy(src, xs_ref.at[pl.ds(pl.multiple_of(dst_row * wr, wr), wr)], sem)

    @pl.when(i == 0)
    def _():
        zero_ref[...] = jnp.zeros_like(zero_ref)
        for phase in ("start", "wait"):
            def pad_rows(ex, c, phase=phase):
                def pad_row(r, c2):
                    cp = row_copy(zero_ref, start_ref[ex] + r)
                    cp.start() if phase == "start" else cp.wait()
                    return c2
                return lax.fori_loop(cnt_ref[ex], pad_ref[ex], pad_row, c)
            lax.fori_loop(0, n_exp, pad_rows, 0)

    def token(t, c):
        src = hp_ref.at[pl.ds(pl.multiple_of(t * wr, wr), wr)]
        for k in range(TOP_K):
            a = t * TOP_K + k
            row_copy(src, start_ref[idx_ref[a]] + pos_ref[a]).start()
        return c

    lax.fori_loop(0, tile, token, 0)
    for _ in range(TOP_K):
        pltpu.make_async_copy(hp_ref, xs_ref.at[pl.ds(0, tile * wr)], sem).wait()


def _dispatch(hp, idx_flat, pos_flat, pad_start, counts, padded, n_tok, n_rows, wr):
    tile = 512
    n_exp = pad_start.shape[0]
    kern = functools.partial(_dispatch_kernel, tile=tile, wr=wr, n_exp=n_exp)
    flat = pl.BlockSpec((tile * TOP_K,), lambda i, *_: (i,), memory_space=pltpu.SMEM)
    return pl.pallas_call(
        kern,
        grid_spec=pltpu.PrefetchScalarGridSpec(
            num_scalar_prefetch=3,
            grid=(n_tok // tile,),
            in_specs=[flat, flat, pl.BlockSpec((tile * wr, LANES), lambda i, *_: (i, 0))],
            out_specs=pl.BlockSpec(memory_space=pl.ANY),
            scratch_shapes=[pltpu.VMEM((wr, LANES), U32), pltpu.SemaphoreType.DMA(())],
        ),
        out_shape=jax.ShapeDtypeStruct((n_rows * wr, LANES), U32),
        compiler_params=_params("arbitrary"),
        name="moe_dispatch",
    )(pad_start, counts, padded, idx_flat, pos_flat, hp)


def _split_kernel(w_ref, o_ref):
    half = PAIR_GROUP // 2
    src = lax.broadcasted_iota(I32, (PAIR_GROUP, PAIR_GROUP), 0)
    dst = lax.broadcasted_iota(I32, (PAIR_GROUP, PAIR_GROUP), 1)
    perm = jnp.where(src == jnp.where(dst < half, 2 * dst, 2 * (dst - half) + 1), 1.0, 0.0).astype(BF16)
    for g in range(w_ref.shape[1] // PAIR_GROUP):
        cols = slice(g * PAIR_GROUP, (g + 1) * PAIR_GROUP)
        o_ref[:, cols] = jnp.dot(w_ref[:, cols].astype(BF16), perm, preferred_element_type=F32).astype(BF16)


def _split_gate_up(w):
    n, d, cols = w.shape
    tk = _tile(d, 1024, 512, 256)
    blk = pl.BlockSpec((None, tk, cols), lambda e, i: (e, i, 0))
    return pl.pallas_call(
        _split_kernel,
        grid=(n, d // tk),
        in_specs=[blk],
        out_specs=blk,
        out_shape=jax.ShapeDtypeStruct((n, d, cols), BF16),
        compiler_params=_params("arbitrary", "arbitrary"),
        name="split_gate_up",
    )(w)


def _expert_kernel(blk_e_ref, used_ref, xs_ref, wgu_ref, bgu_ref, wd_ref, bd_ref, y_ref, act_ref, *, wr):
    j = pl.program_id(0)

    used = used_ref[0]

    def up_stage(slot):
        lo, hi = _unpack_rows(xs_ref, MOE_BLOCK, wr)
        xb = jnp.concatenate([lo.astype(BF16), hi.astype(BF16)], axis=1)
        gu = jnp.dot(xb, wgu_ref[...], preferred_element_type=F32) + bgu_ref[...]
        half = PAIR_GROUP // 2
        groups = gu.shape[1] // PAIR_GROUP
        gate = jnp.concatenate([gu[:, g * PAIR_GROUP:g * PAIR_GROUP + half] for g in range(groups)], axis=1)
        up = jnp.concatenate([gu[:, g * PAIR_GROUP + half:(g + 1) * PAIR_GROUP] for g in range(groups)], axis=1)
        gate = jnp.minimum(gate, SWIGLU_LIMIT)
        up = jnp.clip(up, -SWIGLU_LIMIT, SWIGLU_LIMIT)
        act_ref[slot] = ((up + 1.0) * (gate * jax.nn.sigmoid(SWIGLU_ALPHA * gate))).astype(BF16)

    def down_stage(slot):
        out = jnp.dot(act_ref[slot], wd_ref[...], preferred_element_type=F32) + bd_ref[...]
        _pack_rows(out, y_ref, MOE_BLOCK)

    @pl.when(j == 0)
    def _():
        up_stage(0)

    @pl.when(jnp.logical_and(j > 0, j < used))
    def _():
        slot = j % 2
        down_stage(1 - slot)
        up_stage(slot)

    @pl.when(j == used)
    def _():
        down_stage((j + 1) % 2)


def _experts(xs, blk_e, n_used, w_gu, b_gu, w_down, b_down, layer, n_exp, n_blk, wr):
    _, d, cols = w_gu.shape
    de = cols // 2
    base = layer * n_exp
    in_rows = pl.BlockSpec((MOE_BLOCK * wr, LANES), lambda j, be, nu: (jnp.minimum(j, nu[0] - 1), 0))
    out_rows = pl.BlockSpec((MOE_BLOCK * wr, LANES), lambda j, be, nu: (jnp.clip(j - 1, 0, nu[0] - 1), 0))

    def up_expert(shape):
        return pl.BlockSpec((None,) + shape, lambda j, be, nu: (base + be[jnp.minimum(j, n_blk - 1)], 0, 0))

    def down_expert(shape):
        return pl.BlockSpec((None,) + shape, lambda j, be, nu: (base + be[jnp.maximum(j - 1, 0)], 0, 0))

    kern = functools.partial(_expert_kernel, wr=wr)
    return pl.pallas_call(
        kern,
        grid_spec=pltpu.PrefetchScalarGridSpec(
            num_scalar_prefetch=2,
            grid=(n_blk + 1,),
            in_specs=[in_rows, up_expert((d, cols)), up_expert((1, cols)), down_expert((de, d)), down_expert((1, d))],
            out_specs=out_rows,
            scratch_shapes=[pltpu.VMEM((2, MOE_BLOCK, de), BF16)],
        ),
        out_shape=jax.ShapeDtypeStruct((n_blk * MOE_BLOCK * wr, LANES), U32),
        compiler_params=_params("arbitrary"),
        name="moe_experts",
    )(blk_e, n_used, xs, w_gu, b_gu, w_down, b_down)


def _combine_kernel(start_ref, idx_ref, pos_ref, idx_next_ref, pos_next_ref, x_ref, gates_ref, g2_ref, gn_ref,
                    y_ref, xo_ref, buf0_ref, buf1_ref, sem, *, tile, wr, pitch):
    i = pl.program_id(0)
    last = pl.num_programs(0) - 1
    bufs = (buf0_ref, buf1_ref)

    def gather_token(idx_r, pos_r, s, t):
        for k in range(TOP_K):
            a = t * TOP_K + k
            row = start_ref[idx_r[a]] + pos_r[a]
            pltpu.make_async_copy(y_ref.at[pl.ds(pl.multiple_of(row * wr, wr), wr)],
                                  bufs[s].at[k, pl.ds(t * pitch, wr)], sem.at[s]).start()

    def wait_slot(s):
        for k in range(TOP_K):
            pltpu.make_async_copy(y_ref.at[pl.ds(0, tile * wr)], bufs[s].at[k, pl.ds(0, tile * wr)],
                                  sem.at[s]).wait()

    @pl.when(i == 0)
    def _():
        def first(t, c):
            gather_token(idx_ref, pos_ref, 0, t)
            return c
        lax.fori_loop(0, tile, first, 0)

    def step(s):
        wait_slot(s)
        for t in range(tile):
            gather_token(idx_next_ref, pos_next_ref, 1 - s, t)
        f_lo = f_hi = None
        for k in range(TOP_K):
            lo, hi = _unpack_rows(bufs[s], tile, wr, lead=k, pitch=pitch)
            gk = gates_ref[:, k:k + 1]
            f_lo = lo * gk if f_lo is None else f_lo + lo * gk
            f_hi = hi * gk if f_hi is None else f_hi + hi * gk
        f = jnp.concatenate([f_lo, f_hi], axis=1)
        xo_ref[...] = x_ref[...] + g2_ref[...] * _rms(f, gn_ref[...])

        @pl.when(i == last)
        def _():
            wait_slot(1 - s)

    for s in range(2):
        pl.when(i % 2 == s)(functools.partial(step, s))


def _combine(x, y, idx_flat, pos_flat, pad_start, gates, mod, g_out, n_rows, wr):
    d = x.shape[1]
    tile = ROW_TILE
    n_tiles = n_rows // tile
    row = pl.BlockSpec((tile, d), lambda i, *_: (i, 0))
    flat = pl.BlockSpec((tile * TOP_K,), lambda i, *_: (i,), memory_space=pltpu.SMEM)
    flat_next = pl.BlockSpec((tile * TOP_K,), lambda i, *_: (jnp.minimum(i + 1, n_tiles - 1),),
                             memory_space=pltpu.SMEM)
    g2 = mod.spec(5)
    pitch = wr + GATHER_PITCH_PAD
    buf = pltpu.VMEM((TOP_K, tile * pitch, LANES), U32)
    kern = functools.partial(_combine_kernel, tile=tile, wr=wr, pitch=pitch)
    return pl.pallas_call(
        kern,
        grid_spec=pltpu.PrefetchScalarGridSpec(
            num_scalar_prefetch=1,
            grid=(n_tiles,),
            in_specs=[flat, flat, flat_next, flat_next, row, pl.BlockSpec((tile, TOP_K), lambda i, *_: (i, 0)),
                      pl.BlockSpec(g2.block_shape, lambda i, *_: g2.index_map(i)),
                      pl.BlockSpec((1, d), lambda i, *_: (0, 0)), pl.BlockSpec(memory_space=pl.ANY)],
            out_specs=row,
            scratch_shapes=[buf, buf, pltpu.SemaphoreType.DMA((2,))],
        ),
        out_shape=jax.ShapeDtypeStruct((n_rows, d), F32),
        compiler_params=_params("arbitrary"),
        name="moe_combine",
    )(pad_start, idx_flat, pos_flat, idx_flat, pos_flat, x, gates, mod.table, g_out.reshape(1, d), y)


def _moe(x_new, hp, logits_t, mod, g_out, w_gu, b_gu, w_down, b_down, layer, n_exp, n_rows, wr):
    idx, pos, gates, counts = _router(logits_t)
    counts = counts[:, 0].astype(I32)
    padded = (counts + MOE_BLOCK - 1) // MOE_BLOCK * MOE_BLOCK
    pad_end = jnp.cumsum(padded)
    pad_start = pad_end - padded
    n_blk = -(-(n_rows * TOP_K) // MOE_BLOCK) + n_exp
    blk_first = jnp.arange(n_blk, dtype=I32) * MOE_BLOCK
    blk_e = jnp.minimum(jnp.sum(blk_first[:, None] >= pad_end[None, :], axis=1), n_exp - 1).astype(I32)
    n_used = (pad_end[-1:] // MOE_BLOCK).astype(I32)
    idx_flat, pos_flat = idx.T.reshape(-1), pos.T.reshape(-1)
    xs = _dispatch(hp, idx_flat, pos_flat, pad_start, counts, padded, n_rows, n_blk * MOE_BLOCK, wr)
    y = _experts(xs, blk_e, n_used, w_gu, b_gu, w_down, b_down, layer, n_exp, n_blk, wr)
    return _combine(x_new, y, idx_flat, pos_flat, pad_start, gates.T, mod, g_out, n_rows, wr)


def kernel(x, c, ctx, c_ctx, w_mod, b_mod, norm_g, w_in, gmlp_ln_g, gmlp_ln_b, w_spatial, b_spatial,
           lambda_q, lambda_k, subln_g, w_out, router_w, router_b, w_gate_up, b_gate_up, w_down, b_down):
    batch, seq, d = x.shape
    n_ctx = ctx.shape[1]
    depth = w_in.shape[0]
    in_cols = w_in.shape[2]
    a_width = gmlp_ln_g.shape[1]
    head_dim = lambda_q.shape[2]
    v_dim = subln_g.shape[1]
    heads = (in_cols - 2 * a_width) // (4 * head_dim + v_dim)
    qk_width = heads * 2 * head_dim
    q_off = 2 * a_width
    k_off = q_off + qk_width
    val_off = k_off + qk_width
    n_exp = router_w.shape[2]
    de = w_down.shape[2]
    wr = d // 2 // LANES
    n_lat, n_cx = batch * seq, batch * n_ctx
    n_all = n_lat + n_cx
    assert batch < MOD_ROWS and seq % ROW_TILE == 0 and n_ctx % ROW_TILE == 0
    assert seq % n_ctx == 0 and d % (2 * LANES) == 0

    xa = jnp.concatenate([x.reshape(n_lat, d), ctx.reshape(n_cx, d)], axis=0)
    act = jnp.zeros((MOD_ROWS, d), F32).at[:batch].set(c).at[batch].set(c_ctx)
    mod_table = _modulation(act, w_mod, b_mod).reshape(depth * MOD_ROWS * 6, 1, d)
    tables = _rope_tables(seq, head_dim)
    tm = _tile(n_all, 1024, ROW_TILE)
    assert (2 * de) % PAIR_GROUP == 0

    w_in_b, w_out_b = w_in.astype(BF16), w_out.astype(BF16)
    w_gu = _split_gate_up(w_gate_up.reshape(depth * n_exp, d, 2 * de))
    groups = 2 * de // PAIR_GROUP
    b_gu = b_gate_up.reshape(depth * n_exp, groups, PAIR_GROUP // 2, 2)
    b_gu = b_gu.transpose(0, 1, 3, 2).reshape(depth * n_exp, 1, 2 * de)
    w_dn = w_down.astype(BF16).reshape(depth * n_exp, de, d)
    b_dn = b_down.reshape(depth * n_exp, 1, d)

    for l in range(depth):
        last = l == depth - 1
        lam_init = 0.8 - 0.6 * math.exp(-0.3 * l)
        mod = _Mod(mod_table, l, n_lat, seq, batch, ROW_TILE)
        n_act = n_lat if last else n_all

        h = _norm_mod(xa, norm_g[l, 0], mod, n_all)
        z = _matmul([h], w_in_b, l, n_all, tm, _tile(in_cols, 1024, 512, 256), "in_proj")

        a_mix = _gmlp(z, gmlp_ln_g[l], gmlp_ln_b[l], w_spatial[l], b_spatial[l], n_act)
        qk = _rope(z, tables, n_lat, seq, q_off, 2 * qk_width, head_dim)
        b_mix = _attention_latent(qk, z, k_off, val_off, lambda_q, lambda_k, subln_g[l], l, lam_init,
                                  batch, seq, n_ctx, heads, head_dim, v_dim, min(1024, seq), min(512, seq))
        if not last:
            tqc = min(256, n_ctx)
            ctx_seg = (z, k_off, z, val_off, n_ctx, n_lat // n_ctx)
            b_ctx = _attention(z, q_off, n_lat // tqc, [ctx_seg], lambda_q, lambda_k, subln_g[l], l, lam_init,
                               batch, n_ctx, heads, head_dim, v_dim, tqc)
            b_mix = jnp.concatenate([b_mix, b_ctx], axis=0)
        mixo = _matmul([a_mix, b_mix], w_out_b, l, n_act,
                       _tile(n_act, 1024, ROW_TILE), _tile(d, 512, 256), "out_proj")

        x_new, hp, logits_t = _post_mix(xa, mixo, mod, norm_g[l, 1], norm_g[l, 2], router_w[l], router_b[l], n_act)
        xa = _moe(x_new, hp, logits_t, mod, norm_g[l, 3], w_gu, b_gu, w_dn, b_dn, l, n_exp, n_act, wr)
    return xa[:n_lat].reshape(batch, seq, d)
```
